```python
import math
import jax
import jax.numpy as jnp
from jax import lax
import numpy as np

D_MODEL = 4096
BATCH = 16
SEQ = 256
DEPTH = 2
DEC_BATCH = 2
DEC_SEQ = 1024
PAST_LEN = 512

GRID_W = 64
HEAD_DIM = 128
DIFF_HEADS = D_MODEL // 4 // HEAD_DIM
DIFF_QK_DIM = HEAD_DIM // 2
DIFF_V_DIM = HEAD_DIM
GQA_Q_HEADS = D_MODEL // 2 // HEAD_DIM
GQA_KV_HEADS = GQA_Q_HEADS // 4
GQA_GROUP = GQA_Q_HEADS // GQA_KV_HEADS
GQA_HEAD_DIM = HEAD_DIM
RET_HEADS = D_MODEL // 4 // HEAD_DIM
RET_QK_DIM = HEAD_DIM // 2
RET_V_DIM = HEAD_DIM
RET_CHUNK = 128
MIX_WIDTH = DIFF_HEADS * DIFF_V_DIM + GQA_Q_HEADS * GQA_HEAD_DIM + RET_HEADS * RET_V_DIM
IN_SPLIT_WIDTHS = (DIFF_HEADS * 2 * DIFF_QK_DIM, DIFF_HEADS * 2 * DIFF_QK_DIM, DIFF_HEADS * DIFF_V_DIM,
                   GQA_Q_HEADS * GQA_HEAD_DIM, GQA_KV_HEADS * GQA_HEAD_DIM, GQA_KV_HEADS * GQA_HEAD_DIM,
                   RET_HEADS * RET_QK_DIM, RET_HEADS * RET_QK_DIM, RET_HEADS * RET_V_DIM, RET_HEADS * RET_V_DIM)
IN_WIDTH = sum(IN_SPLIT_WIDTHS)
N_EXPERTS = 32
TOP_K = 4
EXPERT_FF = D_MODEL // 2
SWIGLU_LIMIT = 7.0
SWIGLU_ALPHA = 1.702
EXPERT_BLOCK = 128
Q_BLOCK = 128
ROPE_THETA = 10000.0
NORM_EPS = 1e-6
N_MOD = 6

kernel_name = 'hybrid_diff_gqa_retention_moe_dit_step'


def rms_norm(x, g):
    xf = x.astype(jnp.float32)
    y = xf * lax.rsqrt(jnp.mean(jnp.square(xf), axis=-1, keepdims=True) + NORM_EPS)
    return (y * g.astype(jnp.float32)).astype(x.dtype)


def head_layer_norm(x):
    xf = x.astype(jnp.float32)
    mu = jnp.mean(xf, axis=-1, keepdims=True)
    var = jnp.mean(jnp.square(xf - mu), axis=-1, keepdims=True)
    return (xf - mu) * lax.rsqrt(var + NORM_EPS)


def axial_rope_tables(n_tokens, dim):
    rows = n_tokens // GRID_W
    row = jnp.repeat(jnp.arange(rows), GRID_W).astype(jnp.float32)
    col = jnp.tile(jnp.arange(GRID_W), rows).astype(jnp.float32)
    n_freq = dim // 4
    inv_freq = jnp.power(ROPE_THETA, -jnp.arange(n_freq, dtype=jnp.float32) / n_freq)
    ang = jnp.concatenate([row[:, None] * inv_freq, col[:, None] * inv_freq], axis=-1)
    return jnp.cos(ang), jnp.sin(ang)


def apply_rope(x, cos, sin):
    shape = (1, x.shape[1]) + (1,) * (x.ndim - 3) + (cos.shape[-1],)
    cos = cos.reshape(shape)
    sin = sin.reshape(shape)
    x1, x2 = jnp.split(x.astype(jnp.float32), 2, axis=-1)
    return jnp.concatenate([x1 * cos - x2 * sin, x2 * cos + x1 * sin], axis=-1).astype(x.dtype)


def over_query_blocks(fn, q):
    b, t = q.shape[:2]
    nb = t // Q_BLOCK
    qb = jnp.moveaxis(q.reshape((b, nb, Q_BLOCK) + q.shape[2:]), 1, 0)
    out = lax.map(fn, qb)
    return jnp.moveaxis(out, 0, 1).reshape((b, t) + out.shape[3:])


def diff_attention(q, k, v, lam, subln_g, lam_init):
    scale = DIFF_QK_DIM ** -0.5

    def block(qb):
        s = jnp.einsum('bqhmd,bkhmd->bhmqk', qb, k).astype(jnp.float32) * scale
        p = jax.nn.softmax(s, axis=-1).astype(v.dtype)
        o = jnp.einsum('bhmqk,bkhv->bqhmv', p, v).astype(jnp.float32)
        return (o[..., 0, :] - lam * o[..., 1, :]).astype(v.dtype)

    o = over_query_blocks(block, q)
    o = rms_norm(o, subln_g) * (1.0 - lam_init)
    return o.reshape(o.shape[0], o.shape[1], -1)


def gqa_attention(q, k, v):
    scale = GQA_HEAD_DIM ** -0.5

    def block(qb):
        s = jnp.einsum('bqhgd,bkhd->bhgqk', qb, k).astype(jnp.float32) * scale
        p = jax.nn.softmax(s, axis=-1).astype(v.dtype)
        return jnp.einsum('bhgqk,bkhd->bqhgd', p, v)

    o = over_query_blocks(block, q)
    return o.reshape(o.shape[0], o.shape[1], -1)


def retention_scan(q, k, v, log_decay, state0):
    b, t, h, _ = q.shape
    dv = v.shape[-1]
    n = t // RET_CHUNK

    def chunks(a):
        return a.astype(jnp.float32).reshape(b, n, RET_CHUNK, h, a.shape[-1]).transpose(1, 0, 3, 2, 4)

    lg = log_decay.astype(jnp.float32)
    idx = jnp.arange(RET_CHUNK, dtype=jnp.float32)
    rel = idx[:, None] - idx[None, :]
    intra = jnp.where(rel >= 0, jnp.exp(jnp.maximum(rel, 0.0)[None] * lg[:, None, None]), 0.0)
    cross = jnp.exp((idx + 1.0)[None] * lg[:, None])
    tail = jnp.exp((RET_CHUNK - 1.0 - idx)[None] * lg[:, None])
    carry_decay = jnp.exp(RET_CHUNK * lg)

    def step(state, qkv):
        qi, ki, vi = qkv
        scores = jnp.einsum('bhld,bhmd->bhlm', qi, ki) * intra
        o = (jnp.einsum('bhlm,bhmv->bhlv', scores, vi)
             + jnp.einsum('bhld,bhdv->bhlv', qi, state) * cross[..., None])
        state = carry_decay[:, None, None] * state + jnp.einsum('bhmd,bhmv->bhdv', ki * tail[..., None], vi)
        return state, o

    state, o = lax.scan(step, state0.astype(jnp.float32), (chunks(q), chunks(k), chunks(v)))
    return o.transpose(1, 0, 3, 2, 4).reshape(b, t, h, dv), state


def bidir_retention(q, k, v, lg_f, lg_b, s_f0, s_b0):
    o_f, s_f = retention_scan(q, k, v, lg_f, s_f0)
    o_b, s_b = retention_scan(q[:, ::-1], k[:, ::-1], v[:, ::-1], lg_b, s_b0)
    return o_f + o_b[:, ::-1], s_f, s_b


def project(h, p):
    b, t, _ = h.shape
    proj = jnp.einsum('btd,de->bte', h, p['w_in'])
    offs = []
    acc = 0
    for w in IN_SPLIT_WIDTHS[:-1]:
        acc += w
        offs.append(acc)
    qa, ka, va, qb, kb, vb, qc, kc, vc, gc = jnp.split(proj, offs, axis=-1)
    return {
        'qa': qa.reshape(b, t, DIFF_HEADS, 2, DIFF_QK_DIM),
        'ka': ka.reshape(b, t, DIFF_HEADS, 2, DIFF_QK_DIM),
        'va': va.reshape(b, t, DIFF_HEADS, DIFF_V_DIM),
        'qb': rms_norm(qb.reshape(b, t, GQA_KV_HEADS, GQA_GROUP, GQA_HEAD_DIM), p['q_norm_g']),
        'kb': rms_norm(kb.reshape(b, t, GQA_KV_HEADS, GQA_HEAD_DIM), p['k_norm_g']),
        'vb': vb.reshape(b, t, GQA_KV_HEADS, GQA_HEAD_DIM),
        'qc': qc.reshape(b, t, RET_HEADS, RET_QK_DIM),
        'kc': kc.reshape(b, t, RET_HEADS, RET_QK_DIM) * (RET_QK_DIM ** -0.5),
        'vc': vc.reshape(b, t, RET_HEADS, RET_V_DIM),
        'gc': gc,
    }


def merge_heads(oa, ob, o_ret, gc, w_out_l, dtype):
    b, t = oa.shape[:2]
    oc = head_layer_norm(o_ret).reshape(b, t, -1) * jax.nn.silu(gc.astype(jnp.float32))
    mix = jnp.concatenate([oa.astype(dtype), ob.astype(dtype), oc.astype(dtype)], axis=-1)
    return jnp.einsum('bte,ed->btd', mix, w_out_l)


def context_mixer(h, p):
    t = project(h, p)
    b = h.shape[0]
    oa = diff_attention(t['qa'], t['ka'], t['va'], p['lam'], p['subln_g'], p['lam_init'])
    ob = gqa_attention(t['qb'], t['kb'], t['vb'])
    zeros = jnp.zeros((b, RET_HEADS, RET_QK_DIM, RET_V_DIM), jnp.float32)
    o_ret, s_f, s_b = bidir_retention(t['qc'], t['kc'], t['vc'], p['lg_f'], p['lg_b'], zeros, zeros)
    out = merge_heads(oa, ob, o_ret, t['gc'], p['w_out'], h.dtype)
    return out, (t['ka'], t['va'], t['kb'], t['vb'], s_f, s_b)


def latent_mixer(h, ctx, p):
    k_a_ctx, v_a_ctx, k_b_ctx, v_b_ctx, s_f0, s_b0 = ctx
    t = project(h, p)
    n = h.shape[1]
    cos_s, sin_s = axial_rope_tables(n, DIFF_QK_DIM)
    cos_b, sin_b = axial_rope_tables(n, GQA_HEAD_DIM)
    cos_r, sin_r = axial_rope_tables(n, RET_QK_DIM)
    ka = jnp.concatenate([k_a_ctx.astype(h.dtype), apply_rope(t['ka'], cos_s, sin_s)], axis=1)
    va = jnp.concatenate([v_a_ctx.astype(h.dtype), t['va']], axis=1)
    oa = diff_attention(apply_rope(t['qa'], cos_s, sin_s), ka, va, p['lam'], p['subln_g'], p['lam_init'])
    kb = jnp.concatenate([k_b_ctx.astype(h.dtype), apply_rope(t['kb'], cos_b, sin_b)], axis=1)
    vb = jnp.concatenate([v_b_ctx.astype(h.dtype), t['vb']], axis=1)
    ob = gqa_attention(apply_rope(t['qb'], cos_b, sin_b), kb, vb)
    o_ret, _, _ = bidir_retention(apply_rope(t['qc'], cos_r, sin_r), apply_rope(t['kc'], cos_r, sin_r), t['vc'],
                                  p['lg_f'], p['lg_b'], s_f0, s_b0)
    return merge_heads(oa, ob, o_ret, t['gc'], p['w_out'], h.dtype)


def moe_ffn(h, layer, router_w, router_b, w_gu, b_gu, w_dn, b_dn):
    b, t, d = h.shape
    x = h.reshape(b * t, d)
    n_assign = b * t * TOP_K
    logits = jnp.einsum('td,de->te', x, router_w[layer]).astype(jnp.float32) + router_b[layer].astype(jnp.float32)
    top_logit, top_e = lax.top_k(logits, TOP_K)
    gates = jax.nn.softmax(top_logit, axis=-1)
    flat_e = top_e.reshape(n_assign)
    order = jnp.argsort(flat_e)
    sorted_e = flat_e[order]
    sorted_tok = (order // TOP_K).astype(jnp.int32)
    sorted_gate = gates.reshape(n_assign)[order]
    counts = jnp.bincount(flat_e, length=N_EXPERTS)
    padded = (counts + EXPERT_BLOCK - 1) // EXPERT_BLOCK * EXPERT_BLOCK
    start = jnp.cumsum(counts) - counts
    pad_end = jnp.cumsum(padded)
    pad_start = pad_end - padded
    slot = pad_start[sorted_e] + jnp.arange(n_assign) - start[sorted_e]
    n_blocks = -(-n_assign // EXPERT_BLOCK) + N_EXPERTS
    cap = n_blocks * EXPERT_BLOCK
    slot_tok = jnp.zeros((cap,), jnp.int32).at[slot].set(sorted_tok)
    slot_gate = jnp.zeros((cap,), jnp.float32).at[slot].set(sorted_gate)
    block_e = jnp.minimum(jnp.searchsorted(pad_end, jnp.arange(n_blocks) * EXPERT_BLOCK, side='right'),
                          N_EXPERTS - 1)
    xb = x[slot_tok].reshape(n_blocks, EXPERT_BLOCK, d)

    def expert_block(args):
        xe, e = args
        gu = (jnp.einsum('td,df->tf', xe, w_gu[layer, e]) + b_gu[layer, e]).astype(jnp.float32)
        gate = jnp.minimum(gu[:, :EXPERT_FF], SWIGLU_LIMIT)
        up = jnp.clip(gu[:, EXPERT_FF:], -SWIGLU_LIMIT, SWIGLU_LIMIT)
        act = ((up + 1.0) * gate * jax.nn.sigmoid(SWIGLU_ALPHA * gate)).astype(xe.dtype)
        return jnp.einsum('tf,fd->td', act, w_dn[layer, e]) + b_dn[layer, e]

    yb = lax.map(expert_block, (xb, block_e)).reshape(cap, d)
    y = jnp.zeros_like(x).at[slot_tok].add((yb * slot_gate[:, None]).astype(x.dtype))
    return y.reshape(b, t, d)


def modulation(cond, w_mod_l, b_mod_l, dtype):
    m = (jnp.einsum('nd,de->ne', cond.astype(w_mod_l.dtype), w_mod_l) + b_mod_l).astype(dtype)
    m = m.reshape(m.shape[0], N_MOD, 1, D_MODEL)
    return [m[:, i] for i in range(N_MOD)]


def ada_norm(x, g, shift, scale):
    return rms_norm(x, g) * (1.0 + scale) + shift


def setup_inputs(seed: int = 0) -> dict:
    key = jax.random.key(seed)
    ks = jax.random.split(key, 32)
    f32 = jnp.float32

    def nrm(k, shape, s):
        return jax.random.normal(k, shape, f32) * s

    base_decay = jnp.asarray(np.log(1.0 - 2.0 ** (-5.0 - np.arange(RET_HEADS))), f32)
    return {
        'x_prompt': nrm(ks[0], (BATCH, SEQ, D_MODEL), 1.0),
        'x_sample': nrm(ks[1], (DEC_BATCH, DEC_SEQ, D_MODEL), 1.0),
        'cache_diff_k': nrm(ks[2], (DEC_BATCH, DEPTH, PAST_LEN, DIFF_HEADS, 2, DIFF_QK_DIM), 1.0),
        'cache_diff_v': nrm(ks[3], (DEC_BATCH, DEPTH, PAST_LEN, DIFF_HEADS, DIFF_V_DIM), 1.0),
        'cache_gqa_k': nrm(ks[4], (DEC_BATCH, DEPTH, PAST_LEN, GQA_KV_HEADS, GQA_HEAD_DIM), 1.0),
        'cache_gqa_v': nrm(ks[5], (DEC_BATCH, DEPTH, PAST_LEN, GQA_KV_HEADS, GQA_HEAD_DIM), 1.0),
        'state_ret_fwd': nrm(ks[6], (DEC_BATCH, DEPTH, RET_HEADS, RET_QK_DIM, RET_V_DIM), 0.5),
        'state_ret_bwd': nrm(ks[7], (DEC_BATCH, DEPTH, RET_HEADS, RET_QK_DIM, RET_V_DIM), 0.5),
        'c': nrm(ks[8], (DEC_BATCH, D_MODEL), 1.0),
        'c_ctx': nrm(ks[9], (D_MODEL,), 1.0),
        'w_mod': nrm(ks[10], (DEPTH, D_MODEL, N_MOD * D_MODEL), 0.5 * D_MODEL ** -0.5),
        'b_mod': nrm(ks[11], (DEPTH, N_MOD * D_MODEL), 0.01),
        'norm_mix_g': 1.0 + nrm(ks[12], (DEPTH, D_MODEL), 0.01),
        'norm_ffn_g': 1.0 + nrm(ks[13], (DEPTH, D_MODEL), 0.01),
        'w_in': nrm(ks[14], (DEPTH, D_MODEL, IN_WIDTH), D_MODEL ** -0.5),
        'w_out': nrm(ks[15], (DEPTH, MIX_WIDTH, D_MODEL), MIX_WIDTH ** -0.5),
        'diff_lambda': nrm(ks[16], (DEPTH, 4, DIFF_QK_DIM), 0.1),
        'diff_subln_g': 1.0 + nrm(ks[17], (DEPTH, DIFF_V_DIM), 0.01),
        'gqa_q_norm_g': 1.0 + nrm(ks[18], (DEPTH, GQA_HEAD_DIM), 0.01),
        'gqa_k_norm_g': 1.0 + nrm(ks[19], (DEPTH, GQA_HEAD_DIM), 0.01),
        'ret_log_decay_fwd': base_decay * (1.0 + nrm(ks[20], (DEPTH, RET_HEADS), 0.05)),
        'ret_log_decay_bwd': base_decay * (1.0 + nrm(ks[21], (DEPTH, RET_HEADS), 0.05)),
        'router_w': nrm(ks[22], (DEPTH, D_MODEL, N_EXPERTS), D_MODEL ** -0.5),
        'router_b': nrm(ks[23], (DEPTH, N_EXPERTS), 0.01),
        'moe_w_gate_up': nrm(ks[24], (DEPTH, N_EXPERTS, D_MODEL, 2 * EXPERT_FF), D_MODEL ** -0.5),
        'moe_b_gate_up': nrm(ks[25], (DEPTH, N_EXPERTS, 2 * EXPERT_FF), 0.01),
        'moe_w_down': nrm(ks[26], (DEPTH, N_EXPERTS, EXPERT_FF, D_MODEL), EXPERT_FF ** -0.5),
        'moe_b_down': nrm(ks[27], (DEPTH, N_EXPERTS, D_MODEL), 0.01),
        'final_norm_g': 1.0 + nrm(ks[28], (D_MODEL,), 0.01),
    }


def reference(x_prompt, x_sample, cache_diff_k, cache_diff_v, cache_gqa_k, cache_gqa_v,
              state_ret_fwd, state_ret_bwd, c, c_ctx, w_mod, b_mod, norm_mix_g, norm_ffn_g,
              w_in, w_out, diff_lambda, diff_subln_g, gqa_q_norm_g, gqa_k_norm_g,
              ret_log_decay_fwd, ret_log_decay_bwd, router_w, router_b,
              moe_w_gate_up, moe_b_gate_up, moe_w_down, moe_b_down, final_norm_g):
    cond_ctx = jax.nn.silu(c_ctx.astype(jnp.float32))[None]
    cond_lat = jax.nn.silu(c.astype(jnp.float32))
    xp = x_prompt
    xs = x_sample
    new_dk, new_dv, new_gk, new_gv, new_sf, new_sb = [], [], [], [], [], []
    for layer in range(DEPTH):
        lam_init = 0.8 - 0.6 * math.exp(-0.3 * layer)
        lv = diff_lambda[layer].astype(jnp.float32)
        lam = jnp.exp(jnp.sum(lv[0] * lv[1])) - jnp.exp(jnp.sum(lv[2] * lv[3])) + lam_init
        p = {'w_in': w_in[layer], 'w_out': w_out[layer], 'lam': lam, 'lam_init': lam_init,
             'subln_g': diff_subln_g[layer], 'q_norm_g': gqa_q_norm_g[layer], 'k_norm_g': gqa_k_norm_g[layer],
             'lg_f': ret_log_decay_fwd[layer], 'lg_b': ret_log_decay_bwd[layer]}

        sh1, sc1, g1, sh2, sc2, g2 = modulation(cond_ctx, w_mod[layer], b_mod[layer], xp.dtype)
        mix, ctx = context_mixer(ada_norm(xp, norm_mix_g[layer], sh1, sc1), p)
        xp = xp + g1 * mix
        xp = xp + g2 * moe_ffn(ada_norm(xp, norm_ffn_g[layer], sh2, sc2), layer, router_w, router_b,
                               moe_w_gate_up, moe_b_gate_up, moe_w_down, moe_b_down)
        new_dk.append(ctx[0])
        new_dv.append(ctx[1])
        new_gk.append(ctx[2])
        new_gv.append(ctx[3])
        new_sf.append(ctx[4])
        new_sb.append(ctx[5])

        cached = (cache_diff_k[:, layer], cache_diff_v[:, layer], cache_gqa_k[:, layer], cache_gqa_v[:, layer],
                  state_ret_fwd[:, layer], state_ret_bwd[:, layer])
        sh1, sc1, g1, sh2, sc2, g2 = modulation(cond_lat, w_mod[layer], b_mod[layer], xs.dtype)
        xs = xs + g1 * latent_mixer(ada_norm(xs, norm_mix_g[layer], sh1, sc1), cached, p)
        xs = xs + g2 * moe_ffn(ada_norm(xs, norm_ffn_g[layer], sh2, sc2), layer, router_w, router_b,
                               moe_w_gate_up, moe_b_gate_up, moe_w_down, moe_b_down)

    y_prompt = rms_norm(xp, final_norm_g)
    y_sample = rms_norm(xs, final_norm_g)
    new_diff_k = jnp.stack(new_dk, axis=1)
    new_diff_v = jnp.stack(new_dv, axis=1)
    new_gqa_k = jnp.stack(new_gk, axis=1)
    new_gqa_v = jnp.stack(new_gv, axis=1)
    new_ret_fwd = jnp.stack(new_sf, axis=1)
    new_ret_bwd = jnp.stack(new_sb, axis=1)
    return (y_prompt, y_sample, new_diff_k, new_diff_v, new_gqa_k, new_gqa_v, new_ret_fwd, new_ret_bwd)
```

```python
import functools
import math

import jax
import jax.numpy as jnp
from jax import lax
from jax.experimental import pallas as pl
from jax.experimental.pallas import tpu as pltpu

F32 = jnp.float32
BF16 = jnp.bfloat16

D_MODEL = 4096
BATCH = 16
SEQ = 256
DEPTH = 2
DEC_BATCH = 2
DEC_SEQ = 1024
PAST_LEN = 512
GRID_W = 64
HEAD_DIM = 128
DIFF_HEADS = 8
DIFF_QK_DIM = 64
GQA_Q_HEADS = 16
GQA_KV_HEADS = 4
GQA_GROUP = 4
RET_HEADS = 8
RET_QK_DIM = 64
N_EXPERTS = 32
TOP_K = 4
EXPERT_FF = D_MODEL // 2
SWIGLU_LIMIT = 7.0
SWIGLU_ALPHA = 1.702
ROPE_THETA = 10000.0
NORM_EPS = 1e-6
N_MOD = 6

N_CTX = BATCH * SEQ
N_LAT = DEC_BATCH * DEC_SEQ
N_TOK = N_CTX + N_LAT
N_STREAM = 1 + DEC_BATCH
IN_WIDTH = 9216
MIX_WIDTH = 4096

LANES = 128
COL_QA, COL_KA, COL_VA, COL_QB, COL_KB, COL_VB, COL_QC, COL_KC, COL_VC, COL_GC = 0, 8, 16, 24, 40, 44, 48, 52, 56, 64

ATT_TQ = 256
MOE_ROWS = 1024
MOE_SUB = 256
MOE_TF = 256
MOE_TD = 256
MOE_NF = EXPERT_FF // MOE_TF
MOE_ND = D_MODEL // MOE_TD
GATHER_ROWS = 256
COMBINE_TOK = 128
VMEM_LIMIT = 56 * 1024 * 1024


def _params(*sem):
    return pltpu.CompilerParams(dimension_semantics=sem, vmem_limit_bytes=VMEM_LIMIT)


def _stream_of_row_tile(i, tm):
    r = i * tm
    return jnp.where(r < N_CTX, 0, 1 + (r - N_CTX) // DEC_SEQ)


def _mod_index(layer, which):
    return (layer * N_MOD + which) * N_STREAM


def _mod_kernel(c_ref, w_ref, b_ref, o_ref):
    c = c_ref[...]
    cond = c * (1.0 / (1.0 + jnp.exp(-c)))
    o_ref[0] = jnp.dot(cond.astype(BF16), w_ref[0].astype(BF16), preferred_element_type=F32) + b_ref[0]


def _modulation(cond_raw, w_mod, b_mod):
    tn = 1024
    n_out = N_MOD * D_MODEL
    out = pl.pallas_call(
        _mod_kernel,
        grid=(DEPTH, n_out // tn),
        in_specs=[
            pl.BlockSpec((8, D_MODEL), lambda l, j: (0, 0)),
            pl.BlockSpec((1, D_MODEL, tn), lambda l, j: (l, 0, j)),
            pl.BlockSpec((1, 1, tn), lambda l, j: (l, 0, j)),
        ],
        out_specs=pl.BlockSpec((1, 8, tn), lambda l, j: (l, 0, j)),
        out_shape=jax.ShapeDtypeStruct((DEPTH, 8, n_out), F32),
        compiler_params=_params("arbitrary", "arbitrary"),
        name="modulation",
    )(cond_raw, w_mod, b_mod.reshape(DEPTH, 1, n_out))
    out = out.reshape(DEPTH, 8, N_MOD, D_MODEL)[:, :N_STREAM]
    return out.transpose(0, 2, 1, 3).reshape(DEPTH * N_MOD * N_STREAM, 1, D_MODEL)


def _ada_norm_value(x, g, shift, scale):
    ms = jnp.mean(x * x, axis=-1, keepdims=True)
    y = x * lax.rsqrt(ms + NORM_EPS) * g
    return y * (1.0 + scale) + shift


def _adanorm_kernel(x_ref, g_ref, sh_ref, sc_ref, o_ref):
    o_ref[...] = _ada_norm_value(x_ref[...], g_ref[...], sh_ref[0], sc_ref[0]).astype(o_ref.dtype)


def _adanorm(x, g, modt, layer, which_shift, which_scale):
    tm = 256
    sh0 = _mod_index(layer, which_shift)
    sc0 = _mod_index(layer, which_scale)
    return pl.pallas_call(
        _adanorm_kernel,
        grid=(N_TOK // tm,),
        in_specs=[
            pl.BlockSpec((tm, D_MODEL), lambda i: (i, 0)),
            pl.BlockSpec((1, D_MODEL), lambda i: (0, 0)),
            pl.BlockSpec((1, 1, D_MODEL), lambda i: (sh0 + _stream_of_row_tile(i, tm), 0, 0)),
            pl.BlockSpec((1, 1, D_MODEL), lambda i: (sc0 + _stream_of_row_tile(i, tm), 0, 0)),
        ],
        out_specs=pl.BlockSpec((tm, D_MODEL), lambda i: (i, 0)),
        out_shape=jax.ShapeDtypeStruct((N_TOK, D_MODEL), BF16),
        compiler_params=_params("arbitrary"),
        name="adanorm",
    )(x, g.reshape(1, D_MODEL), modt, modt)


def _mm_kernel(x_ref, w_ref, o_ref, wb_ref):
    @pl.when(pl.program_id(1) == 0)
    def _():
        wb_ref[...] = w_ref[...].astype(BF16)

    o_ref[...] = jnp.dot(x_ref[...], wb_ref[...], preferred_element_type=F32)


def _mm_residual_kernel(x_ref, w_ref, res_ref, gate_ref, o_ref, wb_ref):
    @pl.when(pl.program_id(1) == 0)
    def _():
        wb_ref[...] = w_ref[...].astype(BF16)

    acc = jnp.dot(x_ref[...], wb_ref[...], preferred_element_type=F32)
    o_ref[...] = res_ref[...] + gate_ref[0] * acc


def _in_proj(h, w):
    tm, tn = 512, 512
    k = h.shape[1]
    n = w.shape[1]
    return pl.pallas_call(
        _mm_kernel,
        grid=(n // tn, N_TOK // tm),
        in_specs=[
            pl.BlockSpec((tm, k), lambda j, i: (i, 0)),
            pl.BlockSpec((k, tn), lambda j, i: (0, j)),
        ],
        out_specs=pl.BlockSpec((tm, tn), lambda j, i: (i, j)),
        out_shape=jax.ShapeDtypeStruct((N_TOK, n), F32),
        scratch_shapes=[pltpu.VMEM((k, tn), BF16)],
        compiler_params=_params("arbitrary", "arbitrary"),
        name="in_proj",
    )(h, w)


def _out_proj(mix, w, res, modt, layer, which_gate):
    tm, tn = 512, 512
    k = mix.shape[1]
    n = w.shape[1]
    g0 = _mod_index(layer, which_gate)
    return pl.pallas_call(
        _mm_residual_kernel,
        grid=(n // tn, N_TOK // tm),
        in_specs=[
            pl.BlockSpec((tm, k), lambda j, i: (i, 0)),
            pl.BlockSpec((k, tn), lambda j, i: (0, j)),
            pl.BlockSpec((tm, tn), lambda j, i: (i, j)),
            pl.BlockSpec((1, 1, tn), lambda j, i: (g0 + _stream_of_row_tile(i, tm), 0, j)),
        ],
        out_specs=pl.BlockSpec((tm, tn), lambda j, i: (i, j)),
        out_shape=jax.ShapeDtypeStruct((N_TOK, n), F32),
        scratch_shapes=[pltpu.VMEM((k, tn), BF16)],
        compiler_params=_params("arbitrary", "arbitrary"),
        name="out_proj",
    )(mix, w, res, modt)


def _rope_tables(n_tokens, unit):
    rows = n_tokens // GRID_W
    row = jnp.repeat(jnp.arange(rows), GRID_W).astype(F32)
    col = jnp.tile(jnp.arange(GRID_W), rows).astype(F32)
    n_freq = unit // 4
    inv_freq = jnp.power(ROPE_THETA, -jnp.arange(n_freq, dtype=F32) / n_freq)
    ang = jnp.concatenate([row[:, None] * inv_freq, col[:, None] * inv_freq], axis=-1)
    cos, sin = jnp.cos(ang), jnp.sin(ang)
    zero = jnp.zeros_like(sin)
    reps = LANES // unit
    c = jnp.tile(jnp.concatenate([cos, cos], axis=-1), (1, reps))
    sa = jnp.tile(jnp.concatenate([-sin, zero], axis=-1), (1, reps))
    sb = jnp.tile(jnp.concatenate([zero, sin], axis=-1), (1, reps))
    return c, sa, sb


def _apply_rope(x, c, sa, sb, half):
    return x * c + pltpu.roll(x, LANES - half, 1) * sa + pltpu.roll(x, half, 1) * sb


_NT = (((1,), (1,)), ((), ()))
_TN = (((0,), (0,)), ((), ()))


def _softmax_pv(score_list, value_list):
    m = score_list[0].max(axis=-1, keepdims=True)
    for s in score_list[1:]:
        m = jnp.maximum(m, s.max(axis=-1, keepdims=True))
    acc = None
    den = None
    for s, v in zip(score_list, value_list):
        p = jnp.exp(s - m)
        d = p.sum(axis=-1, keepdims=True)
        o = jnp.dot(p.astype(BF16), v, preferred_element_type=F32)
        acc = o if acc is None else acc + o
        den = d if den is None else den + d
    return acc / den


def _diff_kernel(*refs, lam_init, latent):
    if latent:
        (lamv_ref, g_ref, q_ref, k_ref, v_ref, kc_ref, vc_ref,
         cq_ref, saq_ref, sbq_ref, ck_ref, sak_ref, sbk_ref, o_ref) = refs
    else:
        lamv_ref, g_ref, q_ref, k_ref, v_ref, o_ref = refs
    lv = lamv_ref[...]
    lam = (jnp.exp(jnp.sum(lv[0:1] * lv[1:2], axis=-1, keepdims=True))
           - jnp.exp(jnp.sum(lv[2:3] * lv[3:4], axis=-1, keepdims=True)) + lam_init)
    q = q_ref[...]
    k = k_ref[...]
    if latent:
        half = DIFF_QK_DIM // 2
        q = _apply_rope(q, cq_ref[...], saq_ref[...], sbq_ref[...], half)
        k = _apply_rope(k, ck_ref[...], sak_ref[...], sbk_ref[...], half)
    scale = DIFF_QK_DIM ** -0.5
    lane = lax.broadcasted_iota(jnp.int32, q.shape, 1)
    first = lane < DIFF_QK_DIM
    kb = k.astype(BF16)
    vb = v_ref[...].astype(BF16)
    if latent:
        kcb = kc_ref[...].astype(BF16)
        vcb = vc_ref[...].astype(BF16)
    outs = []
    for m in range(2):
        qm = jnp.where(first if m == 0 else jnp.logical_not(first), q, 0.0).astype(BF16)
        scores, values = [], []
        if latent:
            scores.append(lax.dot_general(qm, kcb, _NT, preferred_element_type=F32) * scale)
            values.append(vcb)
        scores.append(lax.dot_general(qm, kb, _NT, preferred_element_type=F32) * scale)
        values.append(vb)
        outs.append(_softmax_pv(scores, values))
    o = outs[0] - lam * outs[1]
    ms = jnp.mean(o * o, axis=-1, keepdims=True)
    o = o * lax.rsqrt(ms + NORM_EPS) * g_ref[...] * (1.0 - lam_init)
    o_ref[...] = o.astype(o_ref.dtype)


def _diff_attention(proj, lamv, subln_g, lam_init, layer, cache_k, cache_v, tabs64):
    g = subln_g.reshape(1, HEAD_DIM)
    small = [pl.BlockSpec((4, DIFF_QK_DIM), lambda *a: (0, 0)), pl.BlockSpec((1, HEAD_DIM), lambda *a: (0, 0))]
    ctx = pl.pallas_call(
        functools.partial(_diff_kernel, lam_init=lam_init, latent=False),
        grid=(BATCH, DIFF_HEADS),
        in_specs=small + [
            pl.BlockSpec((SEQ, LANES), lambda b, h: (b, COL_QA + h)),
            pl.BlockSpec((SEQ, LANES), lambda b, h: (b, COL_KA + h)),
            pl.BlockSpec((SEQ, LANES), lambda b, h: (b, COL_VA + h)),
        ],
        out_specs=pl.BlockSpec((SEQ, LANES), lambda b, h: (b, h)),
        out_shape=jax.ShapeDtypeStruct((N_CTX, DIFF_HEADS * HEAD_DIM), BF16),
        compiler_params=_params("arbitrary", "arbitrary"),
        name="diff_ctx",
    )(lamv, g, proj, proj, proj)

    nq = DEC_SEQ // ATT_TQ
    q_row = lambda b, qi: (N_CTX // ATT_TQ) + b * nq + qi
    kv_row = lambda b: (N_CTX // DEC_SEQ) + b
    ck = cache_k.reshape(DEC_BATCH, DEPTH, PAST_LEN, DIFF_HEADS * HEAD_DIM)
    cv = cache_v.reshape(DEC_BATCH, DEPTH, PAST_LEN, DIFF_HEADS * HEAD_DIM)
    tab_q = pl.BlockSpec((ATT_TQ, LANES), lambda b, h, qi: (qi, 0))
    tab_k = pl.BlockSpec((DEC_SEQ, LANES), lambda b, h, qi: (0, 0))
    lat = pl.pallas_call(
        functools.partial(_diff_kernel, lam_init=lam_init, latent=True),
        grid=(DEC_BATCH, DIFF_HEADS, nq),
        in_specs=small + [
            pl.BlockSpec((ATT_TQ, LANES), lambda b, h, qi: (q_row(b, qi), COL_QA + h)),
            pl.BlockSpec((DEC_SEQ, LANES), lambda b, h, qi: (kv_row(b), COL_KA + h)),
            pl.BlockSpec((DEC_SEQ, LANES), lambda b, h, qi: (kv_row(b), COL_VA + h)),
            pl.BlockSpec((None, None, PAST_LEN, LANES), lambda b, h, qi: (b, layer, 0, h)),
            pl.BlockSpec((None, None, PAST_LEN, LANES), lambda b, h, qi: (b, layer, 0, h)),
            tab_q, tab_q, tab_q, tab_k, tab_k, tab_k,
        ],
        out_specs=pl.BlockSpec((ATT_TQ, LANES), lambda b, h, qi: (b * nq + qi, h)),
        out_shape=jax.ShapeDtypeStruct((N_LAT, DIFF_HEADS * HEAD_DIM), BF16),
        compiler_params=_params("arbitrary", "arbitrary", "arbitrary"),
        name="diff_lat",
    )(lamv, g, proj, proj, proj, ck, cv, *tabs64, *tabs64)
    return ctx, lat


def _head_rms(x, g):
    ms = jnp.mean(x * x, axis=-1, keepdims=True)
    return x * lax.rsqrt(ms + NORM_EPS) * g


def _gqa_kernel(*refs, latent):
    if latent:
        (qg_ref, kg_ref, q_ref, k_ref, v_ref, kc_ref, vc_ref,
         cq_ref, saq_ref, sbq_ref, ck_ref, sak_ref, sbk_ref, o_ref) = refs
    else:
        qg_ref, kg_ref, q_ref, k_ref, v_ref, o_ref, kn_ref = refs
    half = HEAD_DIM // 2
    kn = _head_rms(k_ref[...], kg_ref[...])
    if latent:
        kn = _apply_rope(kn, ck_ref[...], sak_ref[...], sbk_ref[...], half)
    else:
        kn_ref[...] = kn
    q = q_ref[...]
    tq = q.shape[0]
    heads = []
    for j in range(GQA_GROUP):
        qj = _head_rms(q[:, j * HEAD_DIM:(j + 1) * HEAD_DIM], qg_ref[...])
        if latent:
            qj = _apply_rope(qj, cq_ref[...], saq_ref[...], sbq_ref[...], half)
        heads.append(qj.astype(BF16))
    qs = jnp.concatenate(heads, axis=0)
    scale = HEAD_DIM ** -0.5
    scores, values = [], []
    if latent:
        scores.append(lax.dot_general(qs, kc_ref[...].astype(BF16), _NT, preferred_element_type=F32) * scale)
        values.append(vc_ref[...].astype(BF16))
    scores.append(lax.dot_general(qs, kn.astype(BF16), _NT, preferred_element_type=F32) * scale)
    values.append(v_ref[...].astype(BF16))
    o = _softmax_pv(scores, values)
    for j in range(GQA_GROUP):
        o_ref[:, j * HEAD_DIM:(j + 1) * HEAD_DIM] = o[j * tq:(j + 1) * tq].astype(o_ref.dtype)


def _gqa_attention(proj, q_g, k_g, layer, cache_k, cache_v, tabs128):
    qg = q_g.reshape(1, HEAD_DIM)
    kg = k_g.reshape(1, HEAD_DIM)
    gw = GQA_GROUP * HEAD_DIM
    small = [pl.BlockSpec((1, HEAD_DIM), lambda *a: (0, 0)), pl.BlockSpec((1, HEAD_DIM), lambda *a: (0, 0))]
    qcol = COL_QB * LANES // gw
    ctx, kn = pl.pallas_call(
        functools.partial(_gqa_kernel, latent=False),
        grid=(BATCH, GQA_KV_HEADS),
        in_specs=small + [
            pl.BlockSpec((SEQ, gw), lambda b, g: (b, qcol + g)),
            pl.BlockSpec((SEQ, LANES), lambda b, g: (b, COL_KB + g)),
            pl.BlockSpec((SEQ, LANES), lambda b, g: (b, COL_VB + g)),
        ],
        out_specs=[pl.BlockSpec((SEQ, gw), lambda b, g: (b, g)),
                   pl.BlockSpec((SEQ, LANES), lambda b, g: (b, g))],
        out_shape=[jax.ShapeDtypeStruct((N_CTX, GQA_Q_HEADS * HEAD_DIM), BF16),
                   jax.ShapeDtypeStruct((N_CTX, GQA_KV_HEADS * HEAD_DIM), F32)],
        compiler_params=_params("arbitrary", "arbitrary"),
        name="gqa_ctx",
    )(qg, kg, proj, proj, proj)

    nq = DEC_SEQ // ATT_TQ
    q_row = lambda b, qi: (N_CTX // ATT_TQ) + b * nq + qi
    kv_row = lambda b: (N_CTX // DEC_SEQ) + b
    ck = cache_k.reshape(DEC_BATCH, DEPTH, PAST_LEN, GQA_KV_HEADS * HEAD_DIM)
    cv = cache_v.reshape(DEC_BATCH, DEPTH, PAST_LEN, GQA_KV_HEADS * HEAD_DIM)
    tab_q = pl.BlockSpec((ATT_TQ, LANES), lambda b, g, qi: (qi, 0))
    tab_k = pl.BlockSpec((DEC_SEQ, LANES), lambda b, g, qi: (0, 0))
    lat = pl.pallas_call(
        functools.partial(_gqa_kernel, latent=True),
        grid=(DEC_BATCH, GQA_KV_HEADS, nq),
        in_specs=small + [
            pl.BlockSpec((ATT_TQ, gw), lambda b, g, qi: (q_row(b, qi), qcol + g)),
            pl.BlockSpec((DEC_SEQ, LANES), lambda b, g, qi: (kv_row(b), COL_KB + g)),
            pl.BlockSpec((DEC_SEQ, LANES), lambda b, g, qi: (kv_row(b), COL_VB + g)),
            pl.BlockSpec((None, None, PAST_LEN, LANES), lambda b, g, qi: (b, layer, 0, g)),
            pl.BlockSpec((None, None, PAST_LEN, LANES), lambda b, g, qi: (b, layer, 0, g)),
            tab_q, tab_q, tab_q, tab_k, tab_k, tab_k,
        ],
        out_specs=pl.BlockSpec((ATT_TQ, gw), lambda b, g, qi: (b * nq + qi, g)),
        out_shape=jax.ShapeDtypeStruct((N_LAT, GQA_Q_HEADS * HEAD_DIM), BF16),
        compiler_params=_params("arbitrary", "arbitrary", "arbitrary"),
        name="gqa_lat",
    )(qg, kg, proj, proj, proj, ck, cv, *tabs128, *tabs128)
    return ctx, lat, kn


def _ret_kernel(*refs, seq, latent):
    if latent:
        (lgf_ref, lgb_ref, q_ref, k_ref, v_ref, gc_ref, s0f_ref, s0b_ref,
         cq_ref, saq_ref, sbq_ref, ck_ref, sak_ref, sbk_ref, o_ref) = refs
    else:
        lgf_ref, lgb_ref, q_ref, k_ref, v_ref, gc_ref, o_ref, sf_ref, sb_ref = refs
    hp = pl.program_id(1)
    qi = pl.program_id(2)
    q = q_ref[...]
    k = k_ref[...] * (RET_QK_DIM ** -0.5)
    if latent:
        half = RET_QK_DIM // 2
        q = _apply_rope(q, cq_ref[...], saq_ref[...], sbq_ref[...], half)
        k = _apply_rope(k, ck_ref[...], sak_ref[...], sbk_ref[...], half)
    tq = q.shape[0]
    t_idx = qi * tq + lax.broadcasted_iota(jnp.int32, (tq, seq), 0)
    s_idx = lax.broadcasted_iota(jnp.int32, (tq, seq), 1)
    rel = (t_idx - s_idx).astype(F32)
    t_col = (qi * tq + lax.broadcasted_iota(jnp.int32, (tq, 1), 0)).astype(F32)
    s_col = lax.broadcasted_iota(jnp.int32, (seq, 1), 0).astype(F32)
    lane_q = lax.broadcasted_iota(jnp.int32, q.shape, 1)
    lane_k = lax.broadcasted_iota(jnp.int32, k.shape, 1)
    kb = k.astype(BF16)
    for j in range(2):
        lgf = lgf_ref[2 * hp + j]
        lgb = lgb_ref[2 * hp + j]
        decay = (jnp.where(rel >= 0, jnp.exp(jnp.maximum(rel, 0.0) * lgf), 0.0)
                 + jnp.where(rel <= 0, jnp.exp(jnp.maximum(-rel, 0.0) * lgb), 0.0))
        in_head_q = (lane_q >= j * RET_QK_DIM) & (lane_q < (j + 1) * RET_QK_DIM)
        qj = jnp.where(in_head_q, q, 0.0).astype(BF16)
        vj = v_ref[:, j * HEAD_DIM:(j + 1) * HEAD_DIM].astype(BF16)
        scores = lax.dot_general(qj, kb, _NT, preferred_element_type=F32) * decay
        o = jnp.dot(scores.astype(BF16), vj, preferred_element_type=F32)
        if latent:
            o = o + (jnp.dot(qj, s0f_ref[...].astype(BF16), preferred_element_type=F32)
                     * jnp.exp((t_col + 1.0) * lgf))
            o = o + (jnp.dot(qj, s0b_ref[...].astype(BF16), preferred_element_type=F32)
                     * jnp.exp((seq - t_col) * lgb))
        else:
            in_head_k = (lane_k >= j * RET_QK_DIM) & (lane_k < (j + 1) * RET_QK_DIM)
            kj = jnp.where(in_head_k, k, 0.0)
            kf = (kj * jnp.exp((seq - 1.0 - s_col) * lgf)).astype(BF16)
            kr = (kj * jnp.exp(s_col * lgb)).astype(BF16)
            sf = lax.dot_general(kf, vj, _TN, preferred_element_type=F32)
            sb = lax.dot_general(kr, vj, _TN, preferred_element_type=F32)
            sf_ref[0, j] = sf[j * RET_QK_DIM:(j + 1) * RET_QK_DIM]
            sb_ref[0, j] = sb[j * RET_QK_DIM:(j + 1) * RET_QK_DIM]
        mu = jnp.mean(o, axis=-1, keepdims=True)
        oc = o - mu
        var = jnp.mean(oc * oc, axis=-1, keepdims=True)
        gcj = gc_ref[:, j * HEAD_DIM:(j + 1) * HEAD_DIM]
        silu = gcj * (1.0 / (1.0 + jnp.exp(-gcj)))
        o_ref[:, j * HEAD_DIM:(j + 1) * HEAD_DIM] = (oc * lax.rsqrt(var + NORM_EPS) * silu).astype(o_ref.dtype)


def _retention(proj, lg_f, lg_b, layer, state_f, state_b, tabs64):
    pairs = RET_HEADS // 2
    pw = 2 * HEAD_DIM
    smem = pl.BlockSpec(memory_space=pltpu.SMEM)
    vcol = COL_VC * LANES // pw
    gcol = COL_GC * LANES // pw
    ctx, sf, sb = pl.pallas_call(
        functools.partial(_ret_kernel, seq=SEQ, latent=False),
        grid=(BATCH, pairs, 1),
        in_specs=[
            smem, smem,
            pl.BlockSpec((SEQ, LANES), lambda b, p, qi: (b, COL_QC + p)),
            pl.BlockSpec((SEQ, LANES), lambda b, p, qi: (b, COL_KC + p)),
            pl.BlockSpec((SEQ, pw), lambda b, p, qi: (b, vcol + p)),
            pl.BlockSpec((SEQ, pw), lambda b, p, qi: (b, gcol + p)),
        ],
        out_specs=[
            pl.BlockSpec((SEQ, pw), lambda b, p, qi: (b, p)),
            pl.BlockSpec((1, 2, RET_QK_DIM, HEAD_DIM), lambda b, p, qi: (b, p, 0, 0)),
            pl.BlockSpec((1, 2, RET_QK_DIM, HEAD_DIM), lambda b, p, qi: (b, p, 0, 0)),
        ],
        out_shape=[
            jax.ShapeDtypeStruct((N_CTX, RET_HEADS * HEAD_DIM), BF16),
            jax.ShapeDtypeStruct((BATCH, RET_HEADS, RET_QK_DIM, HEAD_DIM), F32),
            jax.ShapeDtypeStruct((BATCH, RET_HEADS, RET_QK_DIM, HEAD_DIM), F32),
        ],
        compiler_params=_params("arbitrary", "arbitrary", "arbitrary"),
        name="ret_ctx",
    )(lg_f, lg_b, proj, proj, proj, proj)

    nq = DEC_SEQ // ATT_TQ
    q_row = lambda b, qi: (N_CTX // ATT_TQ) + b * nq + qi
    kv_row = lambda b: (N_CTX // DEC_SEQ) + b
    s0f = state_f.reshape(DEC_BATCH, DEPTH, pairs, 2 * RET_QK_DIM, HEAD_DIM)
    s0b = state_b.reshape(DEC_BATCH, DEPTH, pairs, 2 * RET_QK_DIM, HEAD_DIM)
    tab_q = pl.BlockSpec((ATT_TQ, LANES), lambda b, p, qi: (qi, 0))
    tab_k = pl.BlockSpec((DEC_SEQ, LANES), lambda b, p, qi: (0, 0))
    state_spec = pl.BlockSpec((None, None, None, 2 * RET_QK_DIM, HEAD_DIM), lambda b, p, qi: (b, layer, p, 0, 0))
    lat = pl.pallas_call(
        functools.partial(_ret_kernel, seq=DEC_SEQ, latent=True),
        grid=(DEC_BATCH, pairs, nq),
        in_specs=[
            smem, smem,
            pl.BlockSpec((ATT_TQ, LANES), lambda b, p, qi: (q_row(b, qi), COL_QC + p)),
            pl.BlockSpec((DEC_SEQ, LANES), lambda b, p, qi: (kv_row(b), COL_KC + p)),
            pl.BlockSpec((DEC_SEQ, pw), lambda b, p, qi: (kv_row(b), vcol + p)),
            pl.BlockSpec((ATT_TQ, pw), lambda b, p, qi: (q_row(b, qi), gcol + p)),
            state_spec, state_spec,
            tab_q, tab_q, tab_q, tab_k, tab_k, tab_k,
        ],
        out_specs=pl.BlockSpec((ATT_TQ, pw), lambda b, p, qi: (b * nq + qi, p)),
        out_shape=jax.ShapeDtypeStruct((N_LAT, RET_HEADS * HEAD_DIM), BF16),
        compiler_params=_params("arbitrary", "arbitrary", "arbitrary"),
        name="ret_lat",
    )(lg_f, lg_b, proj, proj, proj, proj, s0f, s0b, *tabs64, *tabs64)
    return ctx, lat, sf, sb


def _router_kernel(x_ref, g_ref, sh_ref, sc_ref, rw_ref, rb_ref, h_ref, e_ref, gate_ref):
    h = _ada_norm_value(x_ref[...], g_ref[...], sh_ref[0], sc_ref[0])
    h_ref[...] = h
    hh = h.astype(BF16)
    hl = (h - hh.astype(F32)).astype(BF16)
    w = rw_ref[...]
    wh = w.astype(BF16)
    wl = (w - wh.astype(F32)).astype(BF16)
    logits = (jnp.dot(hh, wh, preferred_element_type=F32) + jnp.dot(hh, wl, preferred_element_type=F32)
              + jnp.dot(hl, wh, preferred_element_type=F32) + rb_ref[...])
    lane = lax.broadcasted_iota(jnp.int32, logits.shape, 1).astype(F32)
    wide = lax.broadcasted_iota(jnp.int32, e_ref.shape, 1)
    e_out = jnp.zeros(e_ref.shape, F32)
    v_out = jnp.zeros(e_ref.shape, F32)
    top0 = None
    den = None
    for kk in range(TOP_K):
        m = jnp.max(logits, axis=-1, keepdims=True)
        idx = jnp.min(jnp.where(logits == m, lane, float(N_EXPERTS)), axis=-1, keepdims=True)
        logits = jnp.where(lane == idx, -jnp.inf, logits)
        if kk == 0:
            top0 = m
        ex = jnp.exp(m - top0)
        den = ex if den is None else den + ex
        e_out = jnp.where(wide == kk, idx, e_out)
        v_out = jnp.where(wide == kk, ex, v_out)
    e_ref[...] = e_out.astype(jnp.int32)
    gate_ref[...] = v_out / den


def _router(x, g, modt, layer, router_w, router_b):
    tm = 256
    sh0 = _mod_index(layer, 3)
    sc0 = _mod_index(layer, 4)
    return pl.pallas_call(
        _router_kernel,
        grid=(N_TOK // tm,),
        in_specs=[
            pl.BlockSpec((tm, D_MODEL), lambda i: (i, 0)),
            pl.BlockSpec((1, D_MODEL), lambda i: (0, 0)),
            pl.BlockSpec((1, 1, D_MODEL), lambda i: (sh0 + _stream_of_row_tile(i, tm), 0, 0)),
            pl.BlockSpec((1, 1, D_MODEL), lambda i: (sc0 + _stream_of_row_tile(i, tm), 0, 0)),
            pl.BlockSpec((None, D_MODEL, N_EXPERTS), lambda i: (layer, 0, 0)),
            pl.BlockSpec((None, 1, N_EXPERTS), lambda i: (layer, 0, 0)),
        ],
        out_specs=[
            pl.BlockSpec((tm, D_MODEL), lambda i: (i, 0)),
            pl.BlockSpec((tm, LANES), lambda i: (i, 0)),
            pl.BlockSpec((tm, LANES), lambda i: (i, 0)),
        ],
        out_shape=[
            jax.ShapeDtypeStruct((N_TOK, D_MODEL), F32),
            jax.ShapeDtypeStruct((N_TOK, LANES), jnp.int32),
            jax.ShapeDtypeStruct((N_TOK, LANES), F32),
        ],
        compiler_params=_params("arbitrary"),
        name="router",
    )(x, g.reshape(1, D_MODEL), modt, modt, router_w, router_b.reshape(DEPTH, 1, N_EXPERTS))


def _route_meta(top_e):
    n_assign = N_TOK * TOP_K
    n_items = -(-n_assign // MOE_ROWS) + N_EXPERTS
    cap = n_items * MOE_ROWS
    flat_e = top_e.reshape(n_assign)
    order = jnp.argsort(flat_e)
    sorted_e = flat_e[order]
    counts = jnp.sum((flat_e[:, None] == jnp.arange(N_EXPERTS)[None, :]).astype(jnp.int32), axis=0)
    padded = (counts + MOE_ROWS - 1) // MOE_ROWS * MOE_ROWS
    start = jnp.cumsum(counts) - counts
    pad_end = jnp.cumsum(padded)
    pad_start = pad_end - padded
    slot_sorted = (pad_start[sorted_e] + jnp.arange(n_assign, dtype=jnp.int32) - start[sorted_e]).astype(jnp.int32)
    slot_tok = jnp.zeros((cap,), jnp.int32).at[slot_sorted].set((order // TOP_K).astype(jnp.int32))
    slot_of = jnp.zeros((n_assign,), jnp.int32).at[order].set(slot_sorted)
    item_start = jnp.arange(n_items, dtype=jnp.int32) * MOE_ROWS
    item_e = jnp.minimum(jnp.searchsorted(pad_end, item_start, side='right'), N_EXPERTS - 1).astype(jnp.int32)
    item_rows = jnp.clip(counts[item_e] - (item_start - pad_start[item_e]), 0, MOE_ROWS).astype(jnp.int32)
    n_act = (pad_end[-1] // MOE_ROWS).astype(jnp.int32).reshape(1)
    return slot_tok, slot_of, item_e, item_rows, n_act, n_items


def _row_copy(src_hbm, row, buf, slot, r, sem):
    return pltpu.make_async_copy(src_hbm.at[pl.ds(row, 1), :], buf.at[slot, pl.ds(r, 1), :], sem.at[slot])


def _gather_kernel(tok_ref, nchunk_ref, h_hbm, o_ref, buf, sem):
    i = pl.program_id(0)
    n_act = nchunk_ref[0]

    def issue(chunk, slot):
        def body(r, carry):
            _row_copy(h_hbm, tok_ref[chunk * GATHER_ROWS + r], buf, slot, r, sem).start()
            return carry
        lax.fori_loop(0, GATHER_ROWS, body, 0)

    @pl.when((i == 0) & (n_act > 0))
    def _():
        issue(0, 0)

    @pl.when(i + 1 < n_act)
    def _():
        issue(i + 1, (i + 1) % 2)

    @pl.when(i < n_act)
    def _():
        slot = i % 2

        def body(r, carry):
            _row_copy(h_hbm, 0, buf, slot, r, sem).wait()
            return carry
        lax.fori_loop(0, GATHER_ROWS, body, 0)
        o_ref[...] = buf[slot].astype(o_ref.dtype)


def _gather_rows(h, slot_tok, n_act_items, cap):
    n_chunks = cap // GATHER_ROWS
    n_act_chunks = n_act_items * (MOE_ROWS // GATHER_ROWS)
    grid_spec = pltpu.PrefetchScalarGridSpec(
        num_scalar_prefetch=2,
        grid=(n_chunks,),
        in_specs=[pl.BlockSpec(memory_space=pl.ANY)],
        out_specs=pl.BlockSpec((GATHER_ROWS, D_MODEL),
                               lambda i, tok, nact: (jnp.minimum(i, jnp.maximum(nact[0] - 1, 0)), 0)),
        scratch_shapes=[pltpu.VMEM((2, GATHER_ROWS, D_MODEL), F32), pltpu.SemaphoreType.DMA((2,))],
    )
    return pl.pallas_call(
        _gather_kernel,
        grid_spec=grid_spec,
        out_shape=jax.ShapeDtypeStruct((cap, D_MODEL), BF16),
        compiler_params=_params("arbitrary"),
        name="moe_gather",
    )(slot_tok, n_act_chunks, h)


def _expert_kernel(ie_ref, rows_ref, nact_ref, x_ref, wg_ref, wu_ref, bg_ref, bu_ref, wd_ref, bd_ref,
                   o_ref, act_ref, wgb_ref, wub_ref, wdb_ref):
    it = pl.program_id(0)
    s = pl.program_id(1)
    active = it < nact_ref[0]
    n_sub = (rows_ref[it] + MOE_SUB - 1) // MOE_SUB

    @pl.when(active & (s < MOE_NF))
    def _():
        wgb_ref[...] = wg_ref[...].astype(BF16)
        wub_ref[...] = wu_ref[...].astype(BF16)
        bg = bg_ref[...]
        bu = bu_ref[...]

        def body(i, carry):
            r0 = pl.multiple_of(i * MOE_SUB, MOE_SUB)
            xs = x_ref[pl.ds(r0, MOE_SUB), :]
            gate = jnp.dot(xs, wgb_ref[...], preferred_element_type=F32) + bg
            up = jnp.dot(xs, wub_ref[...], preferred_element_type=F32) + bu
            gate = jnp.minimum(gate, SWIGLU_LIMIT)
            up = jnp.clip(up, -SWIGLU_LIMIT, SWIGLU_LIMIT)
            a = (up + 1.0) * gate * (1.0 / (1.0 + jnp.exp(-SWIGLU_ALPHA * gate)))
            act_ref[s, pl.ds(r0, MOE_SUB), :] = a.astype(BF16)
            return carry
        lax.fori_loop(0, n_sub, body, 0)

    @pl.when(active & (s >= MOE_NF))
    def _():
        wdb_ref[...] = wd_ref[...].astype(BF16)
        bd = bd_ref[...]

        def body(i, carry):
            r0 = pl.multiple_of(i * MOE_SUB, MOE_SUB)
            acc = jnp.zeros((MOE_SUB, MOE_TD), F32)
            for f in range(MOE_NF):
                acc = acc + jnp.dot(act_ref[f, pl.ds(r0, MOE_SUB), :], wdb_ref[f * MOE_TF:(f + 1) * MOE_TF, :],
                                    preferred_element_type=F32)
            o_ref[pl.ds(r0, MOE_SUB), :] = acc + bd
            return carry
        lax.fori_loop(0, n_sub, body, 0)


def _experts(xs, item_e, item_rows, n_act, n_items, layer, w_gu, b_gu, w_dn, b_dn):
    n_steps = MOE_NF + MOE_ND
    cap = n_items * MOE_ROWS

    def pos(it, s, ie, rows, nact):
        last = jnp.maximum(nact[0] - 1, 0)
        active = it < nact[0]
        itc = jnp.minimum(it, last)
        sc = jnp.where(active, s, n_steps - 1)
        return itc, sc, ie[itc]

    def x_map(it, s, ie, rows, nact):
        itc, _, _ = pos(it, s, ie, rows, nact)
        return (itc, 0)

    def wg_map(it, s, ie, rows, nact):
        _, sc, e = pos(it, s, ie, rows, nact)
        return (layer, e, 0, jnp.minimum(sc, MOE_NF - 1))

    def wu_map(it, s, ie, rows, nact):
        _, sc, e = pos(it, s, ie, rows, nact)
        return (layer, e, 0, MOE_NF + jnp.minimum(sc, MOE_NF - 1))

    def wd_map(it, s, ie, rows, nact):
        _, sc, e = pos(it, s, ie, rows, nact)
        return (layer, e, 0, jnp.clip(sc - MOE_NF, 0, MOE_ND - 1))

    def o_map(it, s, ie, rows, nact):
        itc, sc, _ = pos(it, s, ie, rows, nact)
        return (itc, jnp.clip(sc - MOE_NF, 0, MOE_ND - 1))

    grid_spec = pltpu.PrefetchScalarGridSpec(
        num_scalar_prefetch=3,
        grid=(n_items, n_steps),
        in_specs=[
            pl.BlockSpec((MOE_ROWS, D_MODEL), x_map),
            pl.BlockSpec((None, None, D_MODEL, MOE_TF), wg_map),
            pl.BlockSpec((None, None, D_MODEL, MOE_TF), wu_map),
            pl.BlockSpec((None, None, 1, MOE_TF), wg_map),
            pl.BlockSpec((None, None, 1, MOE_TF), wu_map),
            pl.BlockSpec((None, None, EXPERT_FF, MOE_TD), wd_map),
            pl.BlockSpec((None, None, 1, MOE_TD), wd_map),
        ],
        out_specs=pl.BlockSpec((MOE_ROWS, MOE_TD), o_map),
        scratch_shapes=[
            pltpu.VMEM((MOE_NF, MOE_ROWS, MOE_TF), BF16),
            pltpu.VMEM((D_MODEL, MOE_TF), BF16),
            pltpu.VMEM((D_MODEL, MOE_TF), BF16),
            pltpu.VMEM((EXPERT_FF, MOE_TD), BF16),
        ],
    )
    return pl.pallas_call(
        _expert_kernel,
        grid_spec=grid_spec,
        out_shape=jax.ShapeDtypeStruct((cap, D_MODEL), F32),
        compiler_params=_params("arbitrary", "arbitrary"),
        name="moe_experts",
    )(item_e, item_rows, n_act, xs, w_gu, w_gu,
      b_gu.reshape(DEPTH, N_EXPERTS, 1, 2 * EXPERT_FF), b_gu.reshape(DEPTH, N_EXPERTS, 1, 2 * EXPERT_FF),
      w_dn, b_dn.reshape(DEPTH, N_EXPERTS, 1, D_MODEL))


def _combine_kernel(slot_ref, y_hbm, x_ref, gates_ref, g2_ref, fg_ref, o_ref, buf, sem, *, final_norm):
    i = pl.program_id(0)
    n = pl.num_programs(0)

    def copy(tile, slot, r, kk):
        row = slot_ref[(tile * COMBINE_TOK + r) * TOP_K + kk]
        return pltpu.make_async_copy(y_hbm.at[pl.ds(row, 1), :], buf.at[slot, kk, pl.ds(r, 1), :], sem.at[slot])

    def issue(tile, slot):
        def body(r, carry):
            for kk in range(TOP_K):
                copy(tile, slot, r, kk).start()
            return carry
        lax.fori_loop(0, COMBINE_TOK, body, 0)

    @pl.when(i == 0)
    def _():
        issue(0, 0)

    @pl.when(i + 1 < n)
    def _():
        issue(i + 1, (i + 1) % 2)

    slot = i % 2

    def wait_body(r, carry):
        for kk in range(TOP_K):
            copy(i, slot, r, kk).wait()
        return carry
    lax.fori_loop(0, COMBINE_TOK, wait_body, 0)

    gates = gates_ref[...]
    y = gates[:, 0:1] * buf[slot, 0]
    for kk in range(1, TOP_K):
        y = y + gates[:, kk:kk + 1] * buf[slot, kk]
    out = x_ref[...] + g2_ref[0] * y
    if final_norm:
        ms = jnp.mean(out * out, axis=-1, keepdims=True)
        out = out * lax.rsqrt(ms + NORM_EPS) * fg_ref[...]
    o_ref[...] = out


def _combine(yb, slot_of, x, gates, modt, layer, final_g, final_norm):
    tc = COMBINE_TOK
    g0 = _mod_index(layer, 5)
    grid_spec = pltpu.PrefetchScalarGridSpec(
        num_scalar_prefetch=1,
        grid=(N_TOK // tc,),
        in_specs=[
            pl.BlockSpec(memory_space=pl.ANY),
            pl.BlockSpec((tc, D_MODEL), lambda i, sl: (i, 0)),
            pl.BlockSpec((tc, LANES), lambda i, sl: (i, 0)),
            pl.BlockSpec((1, 1, D_MODEL), lambda i, sl: (g0 + _stream_of_row_tile(i, tc), 0, 0)),
            pl.BlockSpec((1, D_MODEL), lambda i, sl: (0, 0)),
        ],
        out_specs=pl.BlockSpec((tc, D_MODEL), lambda i, sl: (i, 0)),
        scratch_shapes=[pltpu.VMEM((2, TOP_K, tc, D_MODEL), F32), pltpu.SemaphoreType.DMA((2,))],
    )
    return pl.pallas_call(
        functools.partial(_combine_kernel, final_norm=final_norm),
        grid_spec=grid_spec,
        out_shape=jax.ShapeDtypeStruct((N_TOK, D_MODEL), F32),
        compiler_params=_params("arbitrary"),
        name="moe_combine",
    )(slot_of, yb, x, gates, modt, final_g.reshape(1, D_MODEL))


def kernel(x_prompt, x_sample, cache_diff_k, cache_diff_v, cache_gqa_k, cache_gqa_v, state_ret_fwd, state_ret_bwd, c, c_ctx, w_mod, b_mod, norm_mix_g, norm_ffn_g, w_in, w_out, diff_lambda, diff_subln_g, gqa_q_norm_g, gqa_k_norm_g, ret_log_decay_fwd, ret_log_decay_bwd, router_w, router_b, moe_w_gate_up, moe_b_gate_up, moe_w_down, moe_b_down, final_norm_g):
    x = jnp.concatenate([x_prompt.reshape(N_CTX, D_MODEL), x_sample.reshape(N_LAT, D_MODEL)], axis=0)
    cond_raw = jnp.concatenate([c_ctx[None].astype(F32), c.astype(F32),
                                jnp.zeros((8 - N_STREAM, D_MODEL), F32)], axis=0)
    modt = _modulation(cond_raw, w_mod, b_mod)
    tabs64 = _rope_tables(DEC_SEQ, DIFF_QK_DIM)
    tabs128 = _rope_tables(DEC_SEQ, HEAD_DIM)

    new_dk, new_dv, new_gk, new_gv, new_sf, new_sb = [], [], [], [], [], []
    for layer in range(DEPTH):
        lam_init = 0.8 - 0.6 * math.exp(-0.3 * layer)
        h = _adanorm(x, norm_mix_g[layer], modt, layer, 0, 1)
        proj = _in_proj(h, w_in[layer])
        oa_c, oa_l = _diff_attention(proj, diff_lambda[layer], diff_subln_g[layer], lam_init, layer,
                                     cache_diff_k, cache_diff_v, tabs64)
        ob_c, ob_l, kn = _gqa_attention(proj, gqa_q_norm_g[layer], gqa_k_norm_g[layer], layer,
                                        cache_gqa_k, cache_gqa_v, tabs128)
        oc_c, oc_l, sf, sb = _retention(proj, ret_log_decay_fwd[layer], ret_log_decay_bwd[layer], layer,
                                        state_ret_fwd, state_ret_bwd, tabs64)
        mix = jnp.concatenate([jnp.concatenate([oa_c, ob_c, oc_c], axis=1),
                               jnp.concatenate([oa_l, ob_l, oc_l], axis=1)], axis=0)
        x = _out_proj(mix, w_out[layer], x, modt, layer, 2)

        h2, top_e, gates = _router(x, norm_ffn_g[layer], modt, layer, router_w, router_b)
        slot_tok, slot_of, item_e, item_rows, n_act, n_items = _route_meta(top_e[:, :TOP_K])
        xs = _gather_rows(h2, slot_tok, n_act, n_items * MOE_ROWS)
        yb = _experts(xs, item_e, item_rows, n_act, n_items, layer,
                      moe_w_gate_up, moe_b_gate_up, moe_w_down, moe_b_down)
        x = _combine(yb, slot_of, x, gates, modt, layer, final_norm_g, layer == DEPTH - 1)

        pc = proj[:N_CTX]
        new_dk.append(pc[:, COL_KA * LANES:COL_VA * LANES].reshape(BATCH, SEQ, DIFF_HEADS, 2, DIFF_QK_DIM))
        new_dv.append(pc[:, COL_VA * LANES:COL_QB * LANES].reshape(BATCH, SEQ, DIFF_HEADS, HEAD_DIM))
        new_gk.append(kn.reshape(BATCH, SEQ, GQA_KV_HEADS, HEAD_DIM))
        new_gv.append(pc[:, COL_VB * LANES:COL_QC * LANES].reshape(BATCH, SEQ, GQA_KV_HEADS, HEAD_DIM))
        new_sf.append(sf)
        new_sb.append(sb)

    y_prompt = x[:N_CTX].reshape(BATCH, SEQ, D_MODEL)
    y_sample = x[N_CTX:].reshape(DEC_BATCH, DEC_SEQ, D_MODEL)
    return (y_prompt, y_sample, jnp.stack(new_dk, axis=1), jnp.stack(new_dv, axis=1),
            jnp.stack(new_gk, axis=1), jnp.stack(new_gv, axis=1),
            jnp.stack(new_sf, axis=1), jnp.stack(new_sb, axis=1))
```

```python
import functools
import math

import jax
import jax.numpy as jnp
from jax import lax
from jax.experimental import pallas as pl
from jax.experimental.pallas import tpu as pltpu

F32 = jnp.float32
BF16 = jnp.bfloat16

D_MODEL = 4096
BATCH = 16
SEQ = 256
DEPTH = 2
DEC_BATCH = 2
DEC_SEQ = 1024
PAST_LEN = 512
GRID_W = 64
HEAD_DIM = 128
DIFF_HEADS = 8
DIFF_QK_DIM = 64
GQA_Q_HEADS = 16
GQA_KV_HEADS = 4
GQA_GROUP = 4
RET_HEADS = 8
RET_QK_DIM = 64
N_EXPERTS = 32
TOP_K = 4
EXPERT_FF = D_MODEL // 2
SWIGLU_LIMIT = 7.0
SWIGLU_ALPHA = 1.702
ROPE_THETA = 10000.0
NORM_EPS = 1e-6
N_MOD = 6

N_CTX = BATCH * SEQ
N_LAT = DEC_BATCH * DEC_SEQ
N_TOK = N_CTX + N_LAT
N_STREAM = 1 + DEC_BATCH
IN_WIDTH = 9216
MIX_WIDTH = 4096

LANES = 128
COL_QA, COL_KA, COL_VA, COL_QB, COL_KB, COL_VB, COL_QC, COL_KC, COL_VC, COL_GC = 0, 8, 16, 24, 40, 44, 48, 52, 56, 64

ATT_TQ = 256
MOE_ROWS = 1024
MOE_SUB = 256
MOE_TF = 256
MOE_TD = 512
MOE_NF = EXPERT_FF // MOE_TF
MOE_ND = D_MODEL // MOE_TD
MOE_ISSUE = MOE_ROWS // MOE_ND
COMBINE_TOK = 128
VMEM_LIMIT = 60 * 1024 * 1024


def _params(*sem):
    return pltpu.CompilerParams(dimension_semantics=sem, vmem_limit_bytes=VMEM_LIMIT)


def _stream_of_row_tile(i, tm):
    r = i * tm
    return jnp.where(r < N_CTX, 0, 1 + (r - N_CTX) // DEC_SEQ)


def _mod_index(layer, which):
    return (layer * N_MOD + which) * N_STREAM


def _mod_kernel(c_ref, w_ref, b_ref, o_ref):
    c = c_ref[...]
    cond = c * (1.0 / (1.0 + jnp.exp(-c)))
    o_ref[0] = jnp.dot(cond.astype(BF16), w_ref[0].astype(BF16), preferred_element_type=F32) + b_ref[0]


def _modulation(cond_raw, w_mod, b_mod):
    tn = 1024
    n_out = N_MOD * D_MODEL
    out = pl.pallas_call(
        _mod_kernel,
        grid=(DEPTH, n_out // tn),
        in_specs=[
            pl.BlockSpec((8, D_MODEL), lambda l, j: (0, 0)),
            pl.BlockSpec((1, D_MODEL, tn), lambda l, j: (l, 0, j)),
            pl.BlockSpec((1, 1, tn), lambda l, j: (l, 0, j)),
        ],
        out_specs=pl.BlockSpec((1, 8, tn), lambda l, j: (l, 0, j)),
        out_shape=jax.ShapeDtypeStruct((DEPTH, 8, n_out), F32),
        compiler_params=_params("arbitrary", "arbitrary"),
        name="modulation",
    )(cond_raw, w_mod, b_mod.reshape(DEPTH, 1, n_out))
    out = out.reshape(DEPTH, 8, N_MOD, D_MODEL)[:, :N_STREAM]
    return out.transpose(0, 2, 1, 3).reshape(DEPTH * N_MOD * N_STREAM, 1, D_MODEL)


def _ada_norm_value(x, g, shift, scale):
    ms = jnp.mean(x * x, axis=-1, keepdims=True)
    y = x * lax.rsqrt(ms + NORM_EPS) * g
    return y * (1.0 + scale) + shift


def _adanorm_kernel(x_ref, g_ref, sh_ref, sc_ref, o_ref):
    o_ref[...] = _ada_norm_value(x_ref[...], g_ref[...], sh_ref[0], sc_ref[0]).astype(o_ref.dtype)


def _adanorm(x, g, modt, layer, which_shift, which_scale):
    tm = 256
    sh0 = _mod_index(layer, which_shift)
    sc0 = _mod_index(layer, which_scale)
    return pl.pallas_call(
        _adanorm_kernel,
        grid=(N_TOK // tm,),
        in_specs=[
            pl.BlockSpec((tm, D_MODEL), lambda i: (i, 0)),
            pl.BlockSpec((1, D_MODEL), lambda i: (0, 0)),
            pl.BlockSpec((1, 1, D_MODEL), lambda i: (sh0 + _stream_of_row_tile(i, tm), 0, 0)),
            pl.BlockSpec((1, 1, D_MODEL), lambda i: (sc0 + _stream_of_row_tile(i, tm), 0, 0)),
        ],
        out_specs=pl.BlockSpec((tm, D_MODEL), lambda i: (i, 0)),
        out_shape=jax.ShapeDtypeStruct((N_TOK, D_MODEL), BF16),
        compiler_params=_params("arbitrary"),
        name="adanorm",
    )(x, g.reshape(1, D_MODEL), modt, modt)


def _mm_kernel(x_ref, w_ref, o_ref, wb_ref):
    @pl.when(pl.program_id(1) == 0)
    def _():
        wb_ref[...] = w_ref[...].astype(BF16)

    o_ref[...] = jnp.dot(x_ref[...], wb_ref[...], preferred_element_type=F32)


def _mm_residual_kernel(x_ref, w_ref, res_ref, gate_ref, o_ref, wb_ref):
    @pl.when(pl.program_id(1) == 0)
    def _():
        wb_ref[...] = w_ref[...].astype(BF16)

    acc = jnp.dot(x_ref[...], wb_ref[...], preferred_element_type=F32)
    o_ref[...] = res_ref[...] + gate_ref[0] * acc


def _in_proj(h, w):
    tm, tn = 1024, 512
    k = h.shape[1]
    n = w.shape[1]
    return pl.pallas_call(
        _mm_kernel,
        grid=(n // tn, N_TOK // tm),
        in_specs=[
            pl.BlockSpec((tm, k), lambda j, i: (i, 0)),
            pl.BlockSpec((k, tn), lambda j, i: (0, j)),
        ],
        out_specs=pl.BlockSpec((tm, tn), lambda j, i: (i, j)),
        out_shape=jax.ShapeDtypeStruct((N_TOK, n), F32),
        scratch_shapes=[pltpu.VMEM((k, tn), BF16)],
        compiler_params=_params("arbitrary", "arbitrary"),
        name="in_proj",
    )(h, w)


def _out_proj(mix, w, res, modt, layer, which_gate):
    tm, tn = 1024, 512
    k = mix.shape[1]
    n = w.shape[1]
    g0 = _mod_index(layer, which_gate)
    return pl.pallas_call(
        _mm_residual_kernel,
        grid=(n // tn, N_TOK // tm),
        in_specs=[
            pl.BlockSpec((tm, k), lambda j, i: (i, 0)),
            pl.BlockSpec((k, tn), lambda j, i: (0, j)),
            pl.BlockSpec((tm, tn), lambda j, i: (i, j)),
            pl.BlockSpec((1, 1, tn), lambda j, i: (g0 + _stream_of_row_tile(i, tm), 0, j)),
        ],
        out_specs=pl.BlockSpec((tm, tn), lambda j, i: (i, j)),
        out_shape=jax.ShapeDtypeStruct((N_TOK, n), F32),
        scratch_shapes=[pltpu.VMEM((k, tn), BF16)],
        compiler_params=_params("arbitrary", "arbitrary"),
        name="out_proj",
    )(mix, w, res, modt)


def _rope_tables(n_tokens, unit):
    rows = n_tokens // GRID_W
    row = jnp.repeat(jnp.arange(rows), GRID_W).astype(F32)
    col = jnp.tile(jnp.arange(GRID_W), rows).astype(F32)
    n_freq = unit // 4
    inv_freq = jnp.power(ROPE_THETA, -jnp.arange(n_freq, dtype=F32) / n_freq)
    ang = jnp.concatenate([row[:, None] * inv_freq, col[:, None] * inv_freq], axis=-1)
    cos, sin = jnp.cos(ang), jnp.sin(ang)
    zero = jnp.zeros_like(sin)
    reps = LANES // unit
    c = jnp.tile(jnp.concatenate([cos, cos], axis=-1), (1, reps))
    sa = jnp.tile(jnp.concatenate([-sin, zero], axis=-1), (1, reps))
    sb = jnp.tile(jnp.concatenate([zero, sin], axis=-1), (1, reps))
    return c, sa, sb


def _apply_rope(x, c, sa, sb, half):
    return x * c + pltpu.roll(x, LANES - half, 1) * sa + pltpu.roll(x, half, 1) * sb


_ANY = pl.BlockSpec(memory_space=pl.ANY)
_NT = (((1,), (1,)), ((), ()))
_TN = (((0,), (0,)), ((), ()))


def _softmax_pv(score_list, value_list):
    m = score_list[0].max(axis=-1, keepdims=True)
    for s in score_list[1:]:
        m = jnp.maximum(m, s.max(axis=-1, keepdims=True))
    acc = None
    den = None
    for s, v in zip(score_list, value_list):
        p = jnp.exp(s - m)
        d = p.sum(axis=-1, keepdims=True)
        o = jnp.dot(p.astype(BF16), v, preferred_element_type=F32)
        acc = o if acc is None else acc + o
        den = d if den is None else den + d
    return acc / den


def _diff_kernel(*refs, lam_init, latent):
    if latent:
        refs = refs[1:]
        (lamv_ref, g_ref, q_ref, k_ref, v_ref, kc_ref, vc_ref,
         cq_ref, saq_ref, sbq_ref, ck_ref, sak_ref, sbk_ref, o_ref) = refs
    else:
        lamv_ref, g_ref, q_ref, k_ref, v_ref, o_ref = refs
    lv = lamv_ref[...]
    lam = (jnp.exp(jnp.sum(lv[0:1] * lv[1:2], axis=-1, keepdims=True))
           - jnp.exp(jnp.sum(lv[2:3] * lv[3:4], axis=-1, keepdims=True)) + lam_init)
    q = q_ref[...]
    k = k_ref[...]
    if latent:
        half = DIFF_QK_DIM // 2
        q = _apply_rope(q, cq_ref[...], saq_ref[...], sbq_ref[...], half)
        k = _apply_rope(k, ck_ref[...], sak_ref[...], sbk_ref[...], half)
    scale = DIFF_QK_DIM ** -0.5
    lane = lax.broadcasted_iota(jnp.int32, q.shape, 1)
    first = lane < DIFF_QK_DIM
    kb = k.astype(BF16)
    vb = v_ref[...].astype(BF16)
    if latent:
        kcb = kc_ref[...].astype(BF16)
        vcb = vc_ref[...].astype(BF16)
    outs = []
    for m in range(2):
        qm = jnp.where(first if m == 0 else jnp.logical_not(first), q, 0.0).astype(BF16)
        scores, values = [], []
        if latent:
            scores.append(lax.dot_general(qm, kcb, _NT, preferred_element_type=F32) * scale)
            values.append(vcb)
        scores.append(lax.dot_general(qm, kb, _NT, preferred_element_type=F32) * scale)
        values.append(vb)
        outs.append(_softmax_pv(scores, values))
    o = outs[0] - lam * outs[1]
    ms = jnp.mean(o * o, axis=-1, keepdims=True)
    o = o * lax.rsqrt(ms + NORM_EPS) * g_ref[...] * (1.0 - lam_init)
    o_ref[...] = o.astype(o_ref.dtype)


def _diff_attention(proj, lamv, subln_g, lam_init, layer, cache_k, cache_v, tabs64):
    g = subln_g.reshape(1, HEAD_DIM)
    small = [pl.BlockSpec((4, DIFF_QK_DIM), lambda *a: (0, 0)), pl.BlockSpec((1, HEAD_DIM), lambda *a: (0, 0))]
    ctx = pl.pallas_call(
        functools.partial(_diff_kernel, lam_init=lam_init, latent=False),
        grid=(BATCH, DIFF_HEADS),
        in_specs=small + [
            pl.BlockSpec((SEQ, LANES), lambda b, h: (b, COL_QA + h)),
            pl.BlockSpec((SEQ, LANES), lambda b, h: (b, COL_KA + h)),
            pl.BlockSpec((SEQ, LANES), lambda b, h: (b, COL_VA + h)),
        ],
        out_specs=pl.BlockSpec((SEQ, LANES), lambda b, h: (b, h)),
        out_shape=jax.ShapeDtypeStruct((N_TOK, MIX_WIDTH), BF16),
        compiler_params=_params("arbitrary", "arbitrary"),
        name="diff_ctx",
    )(lamv, g, proj, proj, proj)

    nq = DEC_SEQ // ATT_TQ
    q_row = lambda b, qi: (N_CTX // ATT_TQ) + b * nq + qi
    kv_row = lambda b: (N_CTX // DEC_SEQ) + b
    ck = cache_k.reshape(DEC_BATCH, DEPTH, PAST_LEN, DIFF_HEADS * HEAD_DIM)
    cv = cache_v.reshape(DEC_BATCH, DEPTH, PAST_LEN, DIFF_HEADS * HEAD_DIM)
    tab_q = pl.BlockSpec((ATT_TQ, LANES), lambda b, h, qi: (qi, 0))
    tab_k = pl.BlockSpec((DEC_SEQ, LANES), lambda b, h, qi: (0, 0))
    return pl.pallas_call(
        functools.partial(_diff_kernel, lam_init=lam_init, latent=True),
        grid=(DEC_BATCH, DIFF_HEADS, nq),
        in_specs=[_ANY] + small + [
            pl.BlockSpec((ATT_TQ, LANES), lambda b, h, qi: (q_row(b, qi), COL_QA + h)),
            pl.BlockSpec((DEC_SEQ, LANES), lambda b, h, qi: (kv_row(b), COL_KA + h)),
            pl.BlockSpec((DEC_SEQ, LANES), lambda b, h, qi: (kv_row(b), COL_VA + h)),
            pl.BlockSpec((None, None, PAST_LEN, LANES), lambda b, h, qi: (b, layer, 0, h)),
            pl.BlockSpec((None, None, PAST_LEN, LANES), lambda b, h, qi: (b, layer, 0, h)),
            tab_q, tab_q, tab_q, tab_k, tab_k, tab_k,
        ],
        out_specs=pl.BlockSpec((ATT_TQ, LANES), lambda b, h, qi: (q_row(b, qi), h)),
        out_shape=jax.ShapeDtypeStruct((N_TOK, MIX_WIDTH), BF16),
        input_output_aliases={0: 0},
        compiler_params=_params("arbitrary", "arbitrary", "arbitrary"),
        name="diff_lat",
    )(ctx, lamv, g, proj, proj, proj, ck, cv, *tabs64, *tabs64)


def _head_rms(x, g):
    ms = jnp.mean(x * x, axis=-1, keepdims=True)
    return x * lax.rsqrt(ms + NORM_EPS) * g


def _gqa_kernel(*refs, latent):
    refs = refs[1:]
    if latent:
        (qg_ref, kg_ref, q_ref, k_ref, v_ref, kc_ref, vc_ref,
         cq_ref, saq_ref, sbq_ref, ck_ref, sak_ref, sbk_ref, o_ref) = refs
    else:
        qg_ref, kg_ref, q_ref, k_ref, v_ref, o_ref, kn_ref = refs
    half = HEAD_DIM // 2
    kn = _head_rms(k_ref[...], kg_ref[...])
    if latent:
        kn = _apply_rope(kn, ck_ref[...], sak_ref[...], sbk_ref[...], half)
    else:
        kn_ref[...] = kn
    q = q_ref[...]
    tq = q.shape[0]
    heads = []
    for j in range(GQA_GROUP):
        qj = _head_rms(q[:, j * HEAD_DIM:(j + 1) * HEAD_DIM], qg_ref[...])
        if latent:
            qj = _apply_rope(qj, cq_ref[...], saq_ref[...], sbq_ref[...], half)
        heads.append(qj.astype(BF16))
    qs = jnp.concatenate(heads, axis=0)
    scale = HEAD_DIM ** -0.5
    scores, values = [], []
    if latent:
        scores.append(lax.dot_general(qs, kc_ref[...].astype(BF16), _NT, preferred_element_type=F32) * scale)
        values.append(vc_ref[...].astype(BF16))
    scores.append(lax.dot_general(qs, kn.astype(BF16), _NT, preferred_element_type=F32) * scale)
    values.append(v_ref[...].astype(BF16))
    o = _softmax_pv(scores, values)
    for j in range(GQA_GROUP):
        o_ref[:, j * HEAD_DIM:(j + 1) * HEAD_DIM] = o[j * tq:(j + 1) * tq].astype(o_ref.dtype)


def _gqa_attention(mix, proj, q_g, k_g, layer, cache_k, cache_v, tabs128):
    qg = q_g.reshape(1, HEAD_DIM)
    kg = k_g.reshape(1, HEAD_DIM)
    gw = GQA_GROUP * HEAD_DIM
    small = [pl.BlockSpec((1, HEAD_DIM), lambda *a: (0, 0)), pl.BlockSpec((1, HEAD_DIM), lambda *a: (0, 0))]
    qcol = COL_QB * LANES // gw
    mcol = DIFF_HEADS * HEAD_DIM // gw
    mix, kn = pl.pallas_call(
        functools.partial(_gqa_kernel, latent=False),
        grid=(BATCH, GQA_KV_HEADS),
        in_specs=[_ANY] + small + [
            pl.BlockSpec((SEQ, gw), lambda b, g: (b, qcol + g)),
            pl.BlockSpec((SEQ, LANES), lambda b, g: (b, COL_KB + g)),
            pl.BlockSpec((SEQ, LANES), lambda b, g: (b, COL_VB + g)),
        ],
        out_specs=[pl.BlockSpec((SEQ, gw), lambda b, g: (b, mcol + g)),
                   pl.BlockSpec((SEQ, LANES), lambda b, g: (b, g))],
        out_shape=[jax.ShapeDtypeStruct((N_TOK, MIX_WIDTH), BF16),
                   jax.ShapeDtypeStruct((N_CTX, GQA_KV_HEADS * HEAD_DIM), F32)],
        input_output_aliases={0: 0},
        compiler_params=_params("arbitrary", "arbitrary"),
        name="gqa_ctx",
    )(mix, qg, kg, proj, proj, proj)

    nq = DEC_SEQ // ATT_TQ
    q_row = lambda b, qi: (N_CTX // ATT_TQ) + b * nq + qi
    kv_row = lambda b: (N_CTX // DEC_SEQ) + b
    ck = cache_k.reshape(DEC_BATCH, DEPTH, PAST_LEN, GQA_KV_HEADS * HEAD_DIM)
    cv = cache_v.reshape(DEC_BATCH, DEPTH, PAST_LEN, GQA_KV_HEADS * HEAD_DIM)
    tab_q = pl.BlockSpec((ATT_TQ, LANES), lambda b, g, qi: (qi, 0))
    tab_k = pl.BlockSpec((DEC_SEQ, LANES), lambda b, g, qi: (0, 0))
    mix = pl.pallas_call(
        functools.partial(_gqa_kernel, latent=True),
        grid=(DEC_BATCH, GQA_KV_HEADS, nq),
        in_specs=[_ANY] + small + [
            pl.BlockSpec((ATT_TQ, gw), lambda b, g, qi: (q_row(b, qi), qcol + g)),
            pl.BlockSpec((DEC_SEQ, LANES), lambda b, g, qi: (kv_row(b), COL_KB + g)),
            pl.BlockSpec((DEC_SEQ, LANES), lambda b, g, qi: (kv_row(b), COL_VB + g)),
            pl.BlockSpec((None, None, PAST_LEN, LANES), lambda b, g, qi: (b, layer, 0, g)),
            pl.BlockSpec((None, None, PAST_LEN, LANES), lambda b, g, qi: (b, layer, 0, g)),
            tab_q, tab_q, tab_q, tab_k, tab_k, tab_k,
        ],
        out_specs=pl.BlockSpec((ATT_TQ, gw), lambda b, g, qi: (q_row(b, qi), mcol + g)),
        out_shape=jax.ShapeDtypeStruct((N_TOK, MIX_WIDTH), BF16),
        input_output_aliases={0: 0},
        compiler_params=_params("arbitrary", "arbitrary", "arbitrary"),
        name="gqa_lat",
    )(mix, qg, kg, proj, proj, proj, ck, cv, *tabs128, *tabs128)
    return mix, kn


def _ret_kernel(*refs, seq, latent):
    refs = refs[1:]
    if latent:
        (lgf_ref, lgb_ref, q_ref, k_ref, v_ref, gc_ref, s0f_ref, s0b_ref,
         cq_ref, saq_ref, sbq_ref, ck_ref, sak_ref, sbk_ref, o_ref) = refs
    else:
        lgf_ref, lgb_ref, q_ref, k_ref, v_ref, gc_ref, o_ref, sf_ref, sb_ref = refs
    hp = pl.program_id(1)
    qi = pl.program_id(2)
    q = q_ref[...]
    k = k_ref[...] * (RET_QK_DIM ** -0.5)
    if latent:
        half = RET_QK_DIM // 2
        q = _apply_rope(q, cq_ref[...], saq_ref[...], sbq_ref[...], half)
        k = _apply_rope(k, ck_ref[...], sak_ref[...], sbk_ref[...], half)
    tq = q.shape[0]
    t_idx = qi * tq + lax.broadcasted_iota(jnp.int32, (tq, seq), 0)
    s_idx = lax.broadcasted_iota(jnp.int32, (tq, seq), 1)
    rel = (t_idx - s_idx).astype(F32)
    t_col = (qi * tq + lax.broadcasted_iota(jnp.int32, (tq, 1), 0)).astype(F32)
    s_col = lax.broadcasted_iota(jnp.int32, (seq, 1), 0).astype(F32)
    lane_q = lax.broadcasted_iota(jnp.int32, q.shape, 1)
    lane_k = lax.broadcasted_iota(jnp.int32, k.shape, 1)
    kb = k.astype(BF16)
    for j in range(2):
        lgf = lgf_ref[2 * hp + j]
        lgb = lgb_ref[2 * hp + j]
        decay = (jnp.where(rel >= 0, jnp.exp(jnp.maximum(rel, 0.0) * lgf), 0.0)
                 + jnp.where(rel <= 0, jnp.exp(jnp.maximum(-rel, 0.0) * lgb), 0.0))
        in_head_q = (lane_q >= j * RET_QK_DIM) & (lane_q < (j + 1) * RET_QK_DIM)
        qj = jnp.where(in_head_q, q, 0.0).astype(BF16)
        vj = v_ref[:, j * HEAD_DIM:(j + 1) * HEAD_DIM].astype(BF16)
        scores = lax.dot_general(qj, kb, _NT, preferred_element_type=F32) * decay
        o = jnp.dot(scores.astype(BF16), vj, preferred_element_type=F32)
        if latent:
            o = o + (jnp.dot(qj, s0f_ref[...].astype(BF16), preferred_element_type=F32)
                     * jnp.exp((t_col + 1.0) * lgf))
            o = o + (jnp.dot(qj, s0b_ref[...].astype(BF16), preferred_element_type=F32)
                     * jnp.exp((seq - t_col) * lgb))
        else:
            in_head_k = (lane_k >= j * RET_QK_DIM) & (lane_k < (j + 1) * RET_QK_DIM)
            kj = jnp.where(in_head_k, k, 0.0)
            kf = (kj * jnp.exp((seq - 1.0 - s_col) * lgf)).astype(BF16)
            kr = (kj * jnp.exp(s_col * lgb)).astype(BF16)
            sf = lax.dot_general(kf, vj, _TN, preferred_element_type=F32)
            sb = lax.dot_general(kr, vj, _TN, preferred_element_type=F32)
            sf_ref[0, j] = sf[j * RET_QK_DIM:(j + 1) * RET_QK_DIM]
            sb_ref[0, j] = sb[j * RET_QK_DIM:(j + 1) * RET_QK_DIM]
        mu = jnp.mean(o, axis=-1, keepdims=True)
        oc = o - mu
        var = jnp.mean(oc * oc, axis=-1, keepdims=True)
        gcj = gc_ref[:, j * HEAD_DIM:(j + 1) * HEAD_DIM]
        silu = gcj * (1.0 / (1.0 + jnp.exp(-gcj)))
        o_ref[:, j * HEAD_DIM:(j + 1) * HEAD_DIM] = (oc * lax.rsqrt(var + NORM_EPS) * silu).astype(o_ref.dtype)


def _retention(mix, proj, lg_f, lg_b, layer, state_f, state_b, tabs64):
    pairs = RET_HEADS // 2
    pw = 2 * HEAD_DIM
    smem = pl.BlockSpec(memory_space=pltpu.SMEM)
    vcol = COL_VC * LANES // pw
    gcol = COL_GC * LANES // pw
    mcol = (DIFF_HEADS + GQA_Q_HEADS) * HEAD_DIM // pw
    mix, sf, sb = pl.pallas_call(
        functools.partial(_ret_kernel, seq=SEQ, latent=False),
        grid=(BATCH, pairs, 1),
        in_specs=[
            _ANY, smem, smem,
            pl.BlockSpec((SEQ, LANES), lambda b, p, qi: (b, COL_QC + p)),
            pl.BlockSpec((SEQ, LANES), lambda b, p, qi: (b, COL_KC + p)),
            pl.BlockSpec((SEQ, pw), lambda b, p, qi: (b, vcol + p)),
            pl.BlockSpec((SEQ, pw), lambda b, p, qi: (b, gcol + p)),
        ],
        out_specs=[
            pl.BlockSpec((SEQ, pw), lambda b, p, qi: (b, mcol + p)),
            pl.BlockSpec((1, 2, RET_QK_DIM, HEAD_DIM), lambda b, p, qi: (b, p, 0, 0)),
            pl.BlockSpec((1, 2, RET_QK_DIM, HEAD_DIM), lambda b, p, qi: (b, p, 0, 0)),
        ],
        out_shape=[
            jax.ShapeDtypeStruct((N_TOK, MIX_WIDTH), BF16),
            jax.ShapeDtypeStruct((BATCH, RET_HEADS, RET_QK_DIM, HEAD_DIM), F32),
            jax.ShapeDtypeStruct((BATCH, RET_HEADS, RET_QK_DIM, HEAD_DIM), F32),
        ],
        input_output_aliases={0: 0},
        compiler_params=_params("arbitrary", "arbitrary", "arbitrary"),
        name="ret_ctx",
    )(mix, lg_f, lg_b, proj, proj, proj, proj)

    nq = DEC_SEQ // ATT_TQ
    q_row = lambda b, qi: (N_CTX // ATT_TQ) + b * nq + qi
    kv_row = lambda b: (N_CTX // DEC_SEQ) + b
    s0f = state_f.reshape(DEC_BATCH, DEPTH, pairs, 2 * RET_QK_DIM, HEAD_DIM)
    s0b = state_b.reshape(DEC_BATCH, DEPTH, pairs, 2 * RET_QK_DIM, HEAD_DIM)
    tab_q = pl.BlockSpec((ATT_TQ, LANES), lambda b, p, qi: (qi, 0))
    tab_k = pl.BlockSpec((DEC_SEQ, LANES), lambda b, p, qi: (0, 0))
    state_spec = pl.BlockSpec((None, None, None, 2 * RET_QK_DIM, HEAD_DIM), lambda b, p, qi: (b, layer, p, 0, 0))
    mix = pl.pallas_call(
        functools.partial(_ret_kernel, seq=DEC_SEQ, latent=True),
        grid=(DEC_BATCH, pairs, nq),
        in_specs=[
            _ANY, smem, smem,
            pl.BlockSpec((ATT_TQ, LANES), lambda b, p, qi: (q_row(b, qi), COL_QC + p)),
            pl.BlockSpec((DEC_SEQ, LANES), lambda b, p, qi: (kv_row(b), COL_KC + p)),
            pl.BlockSpec((DEC_SEQ, pw), lambda b, p, qi: (kv_row(b), vcol + p)),
            pl.BlockSpec((ATT_TQ, pw), lambda b, p, qi: (q_row(b, qi), gcol + p)),
            state_spec, state_spec,
            tab_q, tab_q, tab_q, tab_k, tab_k, tab_k,
        ],
        out_specs=pl.BlockSpec((ATT_TQ, pw), lambda b, p, qi: (q_row(b, qi), mcol + p)),
        out_shape=jax.ShapeDtypeStruct((N_TOK, MIX_WIDTH), BF16),
        input_output_aliases={0: 0},
        compiler_params=_params("arbitrary", "arbitrary", "arbitrary"),
        name="ret_lat",
    )(mix, lg_f, lg_b, proj, proj, proj, proj, s0f, s0b, *tabs64, *tabs64)
    return mix, sf, sb


def _router_kernel(x_ref, g_ref, sh_ref, sc_ref, rw_ref, rb_ref, h_ref, e_ref, gate_ref):
    h = _ada_norm_value(x_ref[...], g_ref[...], sh_ref[0], sc_ref[0])
    h_ref[...] = h
    hh = h.astype(BF16)
    hl = (h - hh.astype(F32)).astype(BF16)
    w = rw_ref[...]
    wh = w.astype(BF16)
    wl = (w - wh.astype(F32)).astype(BF16)
    logits = (jnp.dot(hh, wh, preferred_element_type=F32) + jnp.dot(hh, wl, preferred_element_type=F32)
              + jnp.dot(hl, wh, preferred_element_type=F32) + rb_ref[...])
    lane = lax.broadcasted_iota(jnp.int32, logits.shape, 1).astype(F32)
    wide = lax.broadcasted_iota(jnp.int32, e_ref.shape, 1)
    e_out = jnp.zeros(e_ref.shape, F32)
    v_out = jnp.zeros(e_ref.shape, F32)
    top0 = None
    den = None
    for kk in range(TOP_K):
        m = jnp.max(logits, axis=-1, keepdims=True)
        idx = jnp.min(jnp.where(logits == m, lane, float(N_EXPERTS)), axis=-1, keepdims=True)
        logits = jnp.where(lane == idx, -jnp.inf, logits)
        if kk == 0:
            top0 = m
        ex = jnp.exp(m - top0)
        den = ex if den is None else den + ex
        e_out = jnp.where(wide == kk, idx, e_out)
        v_out = jnp.where(wide == kk, ex, v_out)
    e_ref[...] = e_out.astype(jnp.int32)
    gate_ref[...] = v_out / den


def _router(x, g, modt, layer, router_w, router_b):
    tm = 256
    sh0 = _mod_index(layer, 3)
    sc0 = _mod_index(layer, 4)
    return pl.pallas_call(
        _router_kernel,
        grid=(N_TOK // tm,),
        in_specs=[
            pl.BlockSpec((tm, D_MODEL), lambda i: (i, 0)),
            pl.BlockSpec((1, D_MODEL), lambda i: (0, 0)),
            pl.BlockSpec((1, 1, D_MODEL), lambda i: (sh0 + _stream_of_row_tile(i, tm), 0, 0)),
            pl.BlockSpec((1, 1, D_MODEL), lambda i: (sc0 + _stream_of_row_tile(i, tm), 0, 0)),
            pl.BlockSpec((None, D_MODEL, N_EXPERTS), lambda i: (layer, 0, 0)),
            pl.BlockSpec((None, 1, N_EXPERTS), lambda i: (layer, 0, 0)),
        ],
        out_specs=[
            pl.BlockSpec((tm, D_MODEL), lambda i: (i, 0)),
            pl.BlockSpec((tm, LANES), lambda i: (i, 0)),
            pl.BlockSpec((tm, LANES), lambda i: (i, 0)),
        ],
        out_shape=[
            jax.ShapeDtypeStruct((N_TOK, D_MODEL), F32),
            jax.ShapeDtypeStruct((N_TOK, LANES), jnp.int32),
            jax.ShapeDtypeStruct((N_TOK, LANES), F32),
        ],
        compiler_params=_params("arbitrary"),
        name="router",
    )(x, g.reshape(1, D_MODEL), modt, modt, router_w, router_b.reshape(DEPTH, 1, N_EXPERTS))


def _route_meta(top_e):
    n_assign = N_TOK * TOP_K
    n_items = -(-n_assign // MOE_ROWS) + N_EXPERTS
    flat_e = top_e.reshape(n_assign)
    order = jnp.argsort(flat_e).astype(jnp.int32)
    inv = jnp.argsort(order).astype(jnp.int32)
    onehot = (flat_e[:, None] == jnp.arange(N_EXPERTS, dtype=jnp.int32)[None, :]).astype(jnp.int32)
    counts = jnp.sum(onehot, axis=0)
    padded = (counts + MOE_ROWS - 1) // MOE_ROWS * MOE_ROWS
    start = jnp.cumsum(counts) - counts
    pad_end = jnp.cumsum(padded)
    pad_start = pad_end - padded
    slot_of = inv + jnp.sum(onehot * (pad_start - start)[None, :], axis=1)
    tok_sorted = order // TOP_K
    item_start = jnp.arange(n_items, dtype=jnp.int32) * MOE_ROWS
    item_e = jnp.minimum(jnp.searchsorted(pad_end, item_start, side='right'), N_EXPERTS - 1).astype(jnp.int32)
    off = item_start - pad_start[item_e]
    item_rows = jnp.clip(counts[item_e] - off, 0, MOE_ROWS).astype(jnp.int32)
    item_rows8 = (item_rows + 7) // 8 * 8
    item_src = jnp.clip(start[item_e] + off, 0, n_assign - 1).astype(jnp.int32)
    n_act = (pad_end[-1] // MOE_ROWS).astype(jnp.int32).reshape(1)
    return tok_sorted, slot_of.astype(jnp.int32), (item_e, item_rows, item_rows8, item_src, n_act), n_items


def _expert_kernel(ie_ref, rows_ref, rows8_ref, src_ref, nact_ref, tok_ref,
                   h_hbm, wg_ref, wu_ref, bg_ref, bu_ref, wd_ref, bd_ref, o_ref, x_ref, act_ref, sem):
    it = pl.program_id(0)
    s = pl.program_id(1)
    n_act = nact_ref[0]
    active = it < n_act
    rows = rows_ref[it]
    n_full = rows // MOE_SUB
    rem = rows - n_full * MOE_SUB
    n_main = n_full + (rem > MOE_SUB // 2).astype(jnp.int32)
    tail0 = pl.multiple_of(n_full * MOE_SUB, MOE_SUB)
    short_tail = (rem > 0) & (rem <= MOE_SUB // 4)
    half_tail = (rem > MOE_SUB // 4) & (rem <= MOE_SUB // 2)

    def row_copy(tok, r):
        return pltpu.make_async_copy(h_hbm.at[pl.ds(tok, 1), :], x_ref.at[pl.ds(r, 1), :], sem.at[0])

    def issue(item, lo, hi):
        base = src_ref[item]
        last = base + rows_ref[item] - 1

        def body(g, carry):
            for u in range(8):
                r = g * 8 + u
                row_copy(tok_ref[jnp.minimum(base + r, last)], r).start()
            return carry
        lax.fori_loop(lo // 8, hi // 8, body, 0)

    @pl.when(active & (s == 0))
    def _():
        @pl.when(it == 0)
        def _():
            x_ref[...] = jnp.zeros(x_ref.shape, F32)
            issue(0, 0, rows8_ref[0])

        def wait_body(g, carry):
            for u in range(8):
                row_copy(0, 0).wait()
            return carry
        lax.fori_loop(0, rows8_ref[it] // 8, wait_body, 0)

    def gate_up(r0, m):
        xs = x_ref[pl.ds(r0, m), :].astype(BF16)
        gate = jnp.dot(xs, wg_ref[...].astype(BF16), preferred_element_type=F32) + bg_ref[...]
        up = jnp.dot(xs, wu_ref[...].astype(BF16), preferred_element_type=F32) + bu_ref[...]
        gate = jnp.minimum(gate, SWIGLU_LIMIT)
        up = jnp.clip(up, -SWIGLU_LIMIT, SWIGLU_LIMIT)
        a = (up + 1.0) * gate * (1.0 / (1.0 + jnp.exp(-SWIGLU_ALPHA * gate)))
        act_ref[s, pl.ds(r0, m), :] = a.astype(BF16)

    def down(r0, m):
        acc = None
        for f in range(MOE_NF):
            d = jnp.dot(act_ref[f, pl.ds(r0, m), :], wd_ref[f * MOE_TF:(f + 1) * MOE_TF, :].astype(BF16),
                        preferred_element_type=F32)
            acc = d if acc is None else acc + d
        o_ref[pl.ds(r0, m), :] = acc + bd_ref[...]

    def run_rows(fn):
        def body(i, carry):
            fn(pl.multiple_of(i * MOE_SUB, MOE_SUB), MOE_SUB)
            return carry
        lax.fori_loop(0, n_main, body, 0)

        @pl.when(short_tail)
        def _():
            fn(tail0, MOE_SUB // 4)

        @pl.when(half_tail)
        def _():
            fn(tail0, MOE_SUB // 2)

    @pl.when(active & (s < MOE_NF))
    def _():
        run_rows(gate_up)

    @pl.when(active & (s >= MOE_NF))
    def _():
        @pl.when(it + 1 < n_act)
        def _():
            d = s - MOE_NF
            issue(it + 1, d * MOE_ISSUE, jnp.minimum((d + 1) * MOE_ISSUE, rows8_ref[it + 1]))
        run_rows(down)


def _experts(h, tok_sorted, item_meta, n_items, layer, w_gu, b_gu, w_dn, b_dn):
    n_steps = MOE_NF + MOE_ND
    cap = n_items * MOE_ROWS

    def pos(it, s, ie, nact):
        last = jnp.maximum(nact[0] - 1, 0)
        itc = jnp.minimum(it, last)
        sc = jnp.where(it < nact[0], s, n_steps - 1)
        return itc, sc, ie[itc]

    def wg_map(it, s, ie, rows, rows8, src, nact, tok):
        _, sc, e = pos(it, s, ie, nact)
        return (layer, e, 0, jnp.minimum(sc, MOE_NF - 1))

    def wu_map(it, s, ie, rows, rows8, src, nact, tok):
        _, sc, e = pos(it, s, ie, nact)
        return (layer, e, 0, MOE_NF + jnp.minimum(sc, MOE_NF - 1))

    def wd_map(it, s, ie, rows, rows8, src, nact, tok):
        _, sc, e = pos(it, s, ie, nact)
        return (layer, e, 0, jnp.clip(sc - MOE_NF, 0, MOE_ND - 1))

    def o_map(it, s, ie, rows, rows8, src, nact, tok):
        itc, sc, _ = pos(it, s, ie, nact)
        return (itc, jnp.clip(sc - MOE_NF, 0, MOE_ND - 1))

    grid_spec = pltpu.PrefetchScalarGridSpec(
        num_scalar_prefetch=6,
        grid=(n_items, n_steps),
        in_specs=[
            _ANY,
            pl.BlockSpec((None, None, D_MODEL, MOE_TF), wg_map),
            pl.BlockSpec((None, None, D_MODEL, MOE_TF), wu_map),
            pl.BlockSpec((None, None, 1, MOE_TF), wg_map),
            pl.BlockSpec((None, None, 1, MOE_TF), wu_map),
            pl.BlockSpec((None, None, EXPERT_FF, MOE_TD), wd_map),
            pl.BlockSpec((None, None, 1, MOE_TD), wd_map),
        ],
        out_specs=pl.BlockSpec((MOE_ROWS, MOE_TD), o_map),
        scratch_shapes=[
            pltpu.VMEM((MOE_ROWS, D_MODEL), F32),
            pltpu.VMEM((MOE_NF, MOE_ROWS, MOE_TF), BF16),
            pltpu.SemaphoreType.DMA((1,)),
        ],
    )
    b_gu4 = b_gu.reshape(DEPTH, N_EXPERTS, 1, 2 * EXPERT_FF)
    return pl.pallas_call(
        _expert_kernel,
        grid_spec=grid_spec,
        out_shape=jax.ShapeDtypeStruct((cap, D_MODEL), F32),
        compiler_params=_params("arbitrary", "arbitrary"),
        name="moe_experts",
    )(*item_meta, tok_sorted, h, w_gu, w_gu, b_gu4, b_gu4, w_dn, b_dn.reshape(DEPTH, N_EXPERTS, 1, D_MODEL))


def _combine_kernel(slot_ref, y_hbm, x_ref, gates_ref, g2_ref, fg_ref, o_ref, buf, sem, *, final_norm):
    i = pl.program_id(0)
    n = pl.num_programs(0)

    def copy(tile, slot, r, kk):
        row = slot_ref[(tile * COMBINE_TOK + r) * TOP_K + kk]
        return pltpu.make_async_copy(y_hbm.at[pl.ds(row, 1), :], buf.at[slot, kk, pl.ds(r, 1), :], sem.at[slot])

    def issue(tile, slot):
        def body(r, carry):
            for kk in range(TOP_K):
                copy(tile, slot, r, kk).start()
            return carry
        lax.fori_loop(0, COMBINE_TOK, body, 0)

    @pl.when(i == 0)
    def _():
        issue(0, 0)

    @pl.when(i + 1 < n)
    def _():
        issue(i + 1, (i + 1) % 2)

    slot = i % 2

    def wait_body(r, carry):
        for kk in range(TOP_K):
            copy(i, slot, r, kk).wait()
        return carry
    lax.fori_loop(0, COMBINE_TOK, wait_body, 0)

    gates = gates_ref[...]
    y = gates[:, 0:1] * buf[slot, 0]
    for kk in range(1, TOP_K):
        y = y + gates[:, kk:kk + 1] * buf[slot, kk]
    out = x_ref[...] + g2_ref[0] * y
    if final_norm:
        ms = jnp.mean(out * out, axis=-1, keepdims=True)
        out = out * lax.rsqrt(ms + NORM_EPS) * fg_ref[...]
    o_ref[...] = out


def _combine(yb, slot_of, x, gates, modt, layer, final_g, final_norm):
    tc = COMBINE_TOK
    g0 = _mod_index(layer, 5)
    grid_spec = pltpu.PrefetchScalarGridSpec(
        num_scalar_prefetch=1,
        grid=(N_TOK // tc,),
        in_specs=[
            pl.BlockSpec(memory_space=pl.ANY),
            pl.BlockSpec((tc, D_MODEL), lambda i, sl: (i, 0)),
            pl.BlockSpec((tc, LANES), lambda i, sl: (i, 0)),
            pl.BlockSpec((1, 1, D_MODEL), lambda i, sl: (g0 + _stream_of_row_tile(i, tc), 0, 0)),
            pl.BlockSpec((1, D_MODEL), lambda i, sl: (0, 0)),
        ],
        out_specs=pl.BlockSpec((tc, D_MODEL), lambda i, sl: (i, 0)),
        scratch_shapes=[pltpu.VMEM((2, TOP_K, tc, D_MODEL), F32), pltpu.SemaphoreType.DMA((2,))],
    )
    return pl.pallas_call(
        functools.partial(_combine_kernel, final_norm=final_norm),
        grid_spec=grid_spec,
        out_shape=jax.ShapeDtypeStruct((N_TOK, D_MODEL), F32),
        compiler_params=_params("arbitrary"),
        name="moe_combine",
    )(slot_of, yb, x, gates, modt, final_g.reshape(1, D_MODEL))


def kernel(x_prompt, x_sample, cache_diff_k, cache_diff_v, cache_gqa_k, cache_gqa_v, state_ret_fwd, state_ret_bwd, c, c_ctx, w_mod, b_mod, norm_mix_g, norm_ffn_g, w_in, w_out, diff_lambda, diff_subln_g, gqa_q_norm_g, gqa_k_norm_g, ret_log_decay_fwd, ret_log_decay_bwd, router_w, router_b, moe_w_gate_up, moe_b_gate_up, moe_w_down, moe_b_down, final_norm_g):
    x = jnp.concatenate([x_prompt.reshape(N_CTX, D_MODEL), x_sample.reshape(N_LAT, D_MODEL)], axis=0)
    cond_raw = jnp.concatenate([c_ctx[None].astype(F32), c.astype(F32),
                                jnp.zeros((8 - N_STREAM, D_MODEL), F32)], axis=0)
    modt = _modulation(cond_raw, w_mod, b_mod)
    tabs64 = _rope_tables(DEC_SEQ, DIFF_QK_DIM)
    tabs128 = _rope_tables(DEC_SEQ, HEAD_DIM)

    new_dk, new_dv, new_gk, new_gv, new_sf, new_sb = [], [], [], [], [], []
    for layer in range(DEPTH):
        lam_init = 0.8 - 0.6 * math.exp(-0.3 * layer)
        h = _adanorm(x, norm_mix_g[layer], modt, layer, 0, 1)
        proj = _in_proj(h, w_in[layer])
        mix = _diff_attention(proj, diff_lambda[layer], diff_subln_g[layer], lam_init, layer,
                              cache_diff_k, cache_diff_v, tabs64)
        mix, kn = _gqa_attention(mix, proj, gqa_q_norm_g[layer], gqa_k_norm_g[layer], layer,
                                 cache_gqa_k, cache_gqa_v, tabs128)
        mix, sf, sb = _retention(mix, proj, ret_log_decay_fwd[layer], ret_log_decay_bwd[layer], layer,
                                 state_ret_fwd, state_ret_bwd, tabs64)
        x = _out_proj(mix, w_out[layer], x, modt, layer, 2)

        h2, top_e, gates = _router(x, norm_ffn_g[layer], modt, layer, router_w, router_b)
        tok_sorted, slot_of, item_meta, n_items = _route_meta(top_e[:, :TOP_K])
        yb = _experts(h2, tok_sorted, item_meta, n_items, layer,
                      moe_w_gate_up, moe_b_gate_up, moe_w_down, moe_b_down)
        x = _combine(yb, slot_of, x, gates, modt, layer, final_norm_g, layer == DEPTH - 1)

        pc = proj[:N_CTX]
        new_dk.append(pc[:, COL_KA * LANES:COL_VA * LANES].reshape(BATCH, SEQ, DIFF_HEADS, 2, DIFF_QK_DIM))
        new_dv.append(pc[:, COL_VA * LANES:COL_QB * LANES].reshape(BATCH, SEQ, DIFF_HEADS, HEAD_DIM))
        new_gk.append(kn.reshape(BATCH, SEQ, GQA_KV_HEADS, HEAD_DIM))
        new_gv.append(pc[:, COL_VB * LANES:COL_QC * LANES].reshape(BATCH, SEQ, GQA_KV_HEADS, HEAD_DIM))
        new_sf.append(sf)
        new_sb.append(sb)

    y_prompt = x[:N_CTX].reshape(BATCH, SEQ, D_MODEL)
    y_sample = x[N_CTX:].reshape(DEC_BATCH, DEC_SEQ, D_MODEL)
    return (y_prompt, y_sample, jnp.stack(new_dk, axis=1), jnp.stack(new_dv, axis=1),
            jnp.stack(new_gk, axis=1), jnp.stack(new_gv, axis=1),
            jnp.stack(new_sf, axis=1), jnp.stack(new_sb, axis=1))
```

```python
import functools
import math

import jax
import jax.numpy as jnp
from jax import lax
from jax.experimental import pallas as pl
from jax.experimental.pallas import tpu as pltpu

F32 = jnp.float32
BF16 = jnp.bfloat16

D_MODEL = 4096
BATCH = 16
SEQ = 256
DEPTH = 2
DEC_BATCH = 2
DEC_SEQ = 1024
PAST_LEN = 512
GRID_W = 64
HEAD_DIM = 128
DIFF_HEADS = 8
DIFF_QK_DIM = 64
GQA_Q_HEADS = 16
GQA_KV_HEADS = 4
GQA_GROUP = 4
RET_HEADS = 8
RET_QK_DIM = 64
N_EXPERTS = 32
TOP_K = 4
EXPERT_FF = D_MODEL // 2
SWIGLU_LIMIT = 7.0
SWIGLU_ALPHA = 1.702
ROPE_THETA = 10000.0
NORM_EPS = 1e-6
N_MOD = 6

N_CTX = BATCH * SEQ
N_LAT = DEC_BATCH * DEC_SEQ
N_TOK = N_CTX + N_LAT
N_STREAM = 1 + DEC_BATCH
IN_WIDTH = 9216
MIX_WIDTH = 4096

LANES = 128
COL_QA, COL_KA, COL_VA, COL_QB, COL_KB, COL_VB, COL_QC, COL_KC, COL_VC, COL_GC = 0, 8, 16, 24, 40, 44, 48, 52, 56, 64

ATT_TQ = 256
DIFF_HB = 2
MOE_ROWS = 1024
MOE_SUB = 256
MOE_TF = 256
MOE_TD = 512
MOE_NF = EXPERT_FF // MOE_TF
MOE_ND = D_MODEL // MOE_TD
MOE_ISSUE = MOE_ROWS // MOE_ND
COMBINE_TOK = 128
VMEM_LIMIT = 60 * 1024 * 1024


def _params(*sem):
    return pltpu.CompilerParams(dimension_semantics=sem, vmem_limit_bytes=VMEM_LIMIT)


def _stream_of_row_tile(i, tm):
    r = i * tm
    return jnp.where(r < N_CTX, 0, 1 + (r - N_CTX) // DEC_SEQ)


def _mod_index(layer, which):
    return (layer * N_MOD + which) * N_STREAM


def _mod_kernel(c_ref, w_ref, b_ref, o_ref):
    c = c_ref[...]
    cond = c * (1.0 / (1.0 + jnp.exp(-c)))
    o_ref[0] = jnp.dot(cond.astype(BF16), w_ref[0].astype(BF16), preferred_element_type=F32) + b_ref[0]


def _modulation(cond_raw, w_mod, b_mod):
    tn = 1024
    n_out = N_MOD * D_MODEL
    out = pl.pallas_call(
        _mod_kernel,
        grid=(DEPTH, n_out // tn),
        in_specs=[
            pl.BlockSpec((8, D_MODEL), lambda l, j: (0, 0)),
            pl.BlockSpec((1, D_MODEL, tn), lambda l, j: (l, 0, j)),
            pl.BlockSpec((1, 1, tn), lambda l, j: (l, 0, j)),
        ],
        out_specs=pl.BlockSpec((1, 8, tn), lambda l, j: (l, 0, j)),
        out_shape=jax.ShapeDtypeStruct((DEPTH, 8, n_out), F32),
        compiler_params=_params("arbitrary", "arbitrary"),
        name="modulation",
    )(cond_raw, w_mod, b_mod.reshape(DEPTH, 1, n_out))
    out = out.reshape(DEPTH, 8, N_MOD, D_MODEL)[:, :N_STREAM]
    return out.transpose(0, 2, 1, 3).reshape(DEPTH * N_MOD * N_STREAM, 1, D_MODEL)


def _ada_norm_value(x, g, shift, scale):
    ms = jnp.mean(x * x, axis=-1, keepdims=True)
    y = x * lax.rsqrt(ms + NORM_EPS) * g
    return y * (1.0 + scale) + shift


def _adanorm_kernel(x_ref, g_ref, sh_ref, sc_ref, o_ref):
    o_ref[...] = _ada_norm_value(x_ref[...], g_ref[...], sh_ref[0], sc_ref[0]).astype(o_ref.dtype)


def _adanorm(x, g, modt, layer, which_shift, which_scale):
    tm = 256
    sh0 = _mod_index(layer, which_shift)
    sc0 = _mod_index(layer, which_scale)
    return pl.pallas_call(
        _adanorm_kernel,
        grid=(N_TOK // tm,),
        in_specs=[
            pl.BlockSpec((tm, D_MODEL), lambda i: (i, 0)),
            pl.BlockSpec((1, D_MODEL), lambda i: (0, 0)),
            pl.BlockSpec((1, 1, D_MODEL), lambda i: (sh0 + _stream_of_row_tile(i, tm), 0, 0)),
            pl.BlockSpec((1, 1, D_MODEL), lambda i: (sc0 + _stream_of_row_tile(i, tm), 0, 0)),
        ],
        out_specs=pl.BlockSpec((tm, D_MODEL), lambda i: (i, 0)),
        out_shape=jax.ShapeDtypeStruct((N_TOK, D_MODEL), BF16),
        compiler_params=_params("arbitrary"),
        name="adanorm",
    )(x, g.reshape(1, D_MODEL), modt, modt)


def _mm_kernel(x_ref, w_ref, o_ref, wb_ref):
    @pl.when(pl.program_id(1) == 0)
    def _():
        wb_ref[...] = w_ref[...].astype(BF16)

    o_ref[...] = jnp.dot(x_ref[...], wb_ref[...], preferred_element_type=F32)


def _mm_residual_kernel(x_ref, w_ref, res_ref, gate_ref, o_ref, wb_ref):
    @pl.when(pl.program_id(1) == 0)
    def _():
        wb_ref[...] = w_ref[...].astype(BF16)

    acc = jnp.dot(x_ref[...], wb_ref[...], preferred_element_type=F32)
    o_ref[...] = res_ref[...] + gate_ref[0] * acc


def _in_proj(h, w):
    tm, tn = 1024, 512
    k = h.shape[1]
    n = w.shape[1]
    return pl.pallas_call(
        _mm_kernel,
        grid=(n // tn, N_TOK // tm),
        in_specs=[
            pl.BlockSpec((tm, k), lambda j, i: (i, 0)),
            pl.BlockSpec((k, tn), lambda j, i: (0, j)),
        ],
        out_specs=pl.BlockSpec((tm, tn), lambda j, i: (i, j)),
        out_shape=jax.ShapeDtypeStruct((N_TOK, n), F32),
        scratch_shapes=[pltpu.VMEM((k, tn), BF16)],
        compiler_params=_params("arbitrary", "arbitrary"),
        name="in_proj",
    )(h, w)


def _out_proj(mix, w, res, modt, layer, which_gate):
    tm, tn = 1024, 512
    k = mix.shape[1]
    n = w.shape[1]
    g0 = _mod_index(layer, which_gate)
    return pl.pallas_call(
        _mm_residual_kernel,
        grid=(n // tn, N_TOK // tm),
        in_specs=[
            pl.BlockSpec((tm, k), lambda j, i: (i, 0)),
            pl.BlockSpec((k, tn), lambda j, i: (0, j)),
            pl.BlockSpec((tm, tn), lambda j, i: (i, j)),
            pl.BlockSpec((1, 1, tn), lambda j, i: (g0 + _stream_of_row_tile(i, tm), 0, j)),
        ],
        out_specs=pl.BlockSpec((tm, tn), lambda j, i: (i, j)),
        out_shape=jax.ShapeDtypeStruct((N_TOK, n), F32),
        scratch_shapes=[pltpu.VMEM((k, tn), BF16)],
        compiler_params=_params("arbitrary", "arbitrary"),
        name="out_proj",
    )(mix, w, res, modt)


def _rope_tables(n_tokens, unit):
    rows = n_tokens // GRID_W
    row = jnp.repeat(jnp.arange(rows), GRID_W).astype(F32)
    col = jnp.tile(jnp.arange(GRID_W), rows).astype(F32)
    n_freq = unit // 4
    inv_freq = jnp.power(ROPE_THETA, -jnp.arange(n_freq, dtype=F32) / n_freq)
    ang = jnp.concatenate([row[:, None] * inv_freq, col[:, None] * inv_freq], axis=-1)
    cos, sin = jnp.cos(ang), jnp.sin(ang)
    zero = jnp.zeros_like(sin)
    reps = LANES // unit
    c = jnp.tile(jnp.concatenate([cos, cos], axis=-1), (1, reps))
    sa = jnp.tile(jnp.concatenate([-sin, zero], axis=-1), (1, reps))
    sb = jnp.tile(jnp.concatenate([zero, sin], axis=-1), (1, reps))
    return c, sa, sb


def _apply_rope(x, c, sa, sb, half):
    return x * c + pltpu.roll(x, LANES - half, 1) * sa + pltpu.roll(x, half, 1) * sb


_ANY = pl.BlockSpec(memory_space=pl.ANY)
_NT = (((1,), (1,)), ((), ()))
_TN = (((0,), (0,)), ((), ()))


def _softmax_pv(score_list, value_list):
    m = score_list[0].max(axis=-1, keepdims=True)
    for s in score_list[1:]:
        m = jnp.maximum(m, s.max(axis=-1, keepdims=True))
    acc = None
    den = None
    for s, v in zip(score_list, value_list):
        p = jnp.exp(s - m)
        d = p.sum(axis=-1, keepdims=True)
        o = jnp.dot(p.astype(BF16), v, preferred_element_type=F32)
        acc = o if acc is None else acc + o
        den = d if den is None else den + d
    return acc / den


def _diff_kernel(*refs, lam_init, latent):
    if latent:
        refs = refs[1:]
        (lamv_ref, g_ref, q_ref, k_ref, v_ref, kc_ref, vc_ref,
         cq_ref, saq_ref, sbq_ref, ck_ref, sak_ref, sbk_ref, o_ref) = refs
    else:
        lamv_ref, g_ref, q_ref, k_ref, v_ref, o_ref = refs
    lv = lamv_ref[...]
    lam = (jnp.exp(jnp.sum(lv[0:1] * lv[1:2], axis=-1, keepdims=True))
           - jnp.exp(jnp.sum(lv[2:3] * lv[3:4], axis=-1, keepdims=True)) + lam_init)
    scale = DIFF_QK_DIM ** -0.5
    half = DIFF_QK_DIM // 2
    for hh in range(DIFF_HB):
        cols = slice(hh * LANES, (hh + 1) * LANES)
        q = q_ref[:, cols]
        k = k_ref[:, cols]
        if latent:
            q = _apply_rope(q, cq_ref[...], saq_ref[...], sbq_ref[...], half)
            k = _apply_rope(k, ck_ref[...], sak_ref[...], sbk_ref[...], half)
        lane = lax.broadcasted_iota(jnp.int32, q.shape, 1)
        first = lane < DIFF_QK_DIM
        kb = k.astype(BF16)
        vb = v_ref[:, cols].astype(BF16)
        if latent:
            kcb = kc_ref[:, cols].astype(BF16)
            vcb = vc_ref[:, cols].astype(BF16)
        outs = []
        for m in range(2):
            qm = jnp.where(first if m == 0 else jnp.logical_not(first), q, 0.0).astype(BF16)
            scores, values = [], []
            if latent:
                scores.append(lax.dot_general(qm, kcb, _NT, preferred_element_type=F32) * scale)
                values.append(vcb)
            scores.append(lax.dot_general(qm, kb, _NT, preferred_element_type=F32) * scale)
            values.append(vb)
            outs.append(_softmax_pv(scores, values))
        o = outs[0] - lam * outs[1]
        ms = jnp.mean(o * o, axis=-1, keepdims=True)
        o = o * lax.rsqrt(ms + NORM_EPS) * g_ref[...] * (1.0 - lam_init)
        o_ref[:, cols] = o.astype(o_ref.dtype)


def _diff_attention(proj, lamv, subln_g, lam_init, layer, cache_k, cache_v, tabs64):
    g = subln_g.reshape(1, HEAD_DIM)
    hw = DIFF_HB * LANES
    small = [pl.BlockSpec((4, DIFF_QK_DIM), lambda *a: (0, 0)), pl.BlockSpec((1, HEAD_DIM), lambda *a: (0, 0))]
    ctx = pl.pallas_call(
        functools.partial(_diff_kernel, lam_init=lam_init, latent=False),
        grid=(BATCH, DIFF_HEADS // DIFF_HB),
        in_specs=small + [
            pl.BlockSpec((SEQ, hw), lambda b, h: (b, COL_QA // DIFF_HB + h)),
            pl.BlockSpec((SEQ, hw), lambda b, h: (b, COL_KA // DIFF_HB + h)),
            pl.BlockSpec((SEQ, hw), lambda b, h: (b, COL_VA // DIFF_HB + h)),
        ],
        out_specs=pl.BlockSpec((SEQ, hw), lambda b, h: (b, h)),
        out_shape=jax.ShapeDtypeStruct((N_TOK, MIX_WIDTH), BF16),
        compiler_params=_params("arbitrary", "arbitrary"),
        name="diff_ctx",
    )(lamv, g, proj, proj, proj)

    nq = DEC_SEQ // ATT_TQ
    q_row = lambda b, qi: (N_CTX // ATT_TQ) + b * nq + qi
    kv_row = lambda b: (N_CTX // DEC_SEQ) + b
    ck = cache_k.reshape(DEC_BATCH, DEPTH, PAST_LEN, DIFF_HEADS * HEAD_DIM)
    cv = cache_v.reshape(DEC_BATCH, DEPTH, PAST_LEN, DIFF_HEADS * HEAD_DIM)
    tab_q = pl.BlockSpec((ATT_TQ, LANES), lambda b, h, qi: (qi, 0))
    tab_k = pl.BlockSpec((DEC_SEQ, LANES), lambda b, h, qi: (0, 0))
    return pl.pallas_call(
        functools.partial(_diff_kernel, lam_init=lam_init, latent=True),
        grid=(DEC_BATCH, DIFF_HEADS // DIFF_HB, nq),
        in_specs=[_ANY] + small + [
            pl.BlockSpec((ATT_TQ, hw), lambda b, h, qi: (q_row(b, qi), COL_QA // DIFF_HB + h)),
            pl.BlockSpec((DEC_SEQ, hw), lambda b, h, qi: (kv_row(b), COL_KA // DIFF_HB + h)),
            pl.BlockSpec((DEC_SEQ, hw), lambda b, h, qi: (kv_row(b), COL_VA // DIFF_HB + h)),
            pl.BlockSpec((None, None, PAST_LEN, hw), lambda b, h, qi: (b, layer, 0, h)),
            pl.BlockSpec((None, None, PAST_LEN, hw), lambda b, h, qi: (b, layer, 0, h)),
            tab_q, tab_q, tab_q, tab_k, tab_k, tab_k,
        ],
        out_specs=pl.BlockSpec((ATT_TQ, hw), lambda b, h, qi: (q_row(b, qi), h)),
        out_shape=jax.ShapeDtypeStruct((N_TOK, MIX_WIDTH), BF16),
        input_output_aliases={0: 0},
        compiler_params=_params("arbitrary", "arbitrary", "arbitrary"),
        name="diff_lat",
    )(ctx, lamv, g, proj, proj, proj, ck, cv, *tabs64, *tabs64)


def _head_rms(x, g):
    ms = jnp.mean(x * x, axis=-1, keepdims=True)
    return x * lax.rsqrt(ms + NORM_EPS) * g


def _gqa_kernel(*refs, latent):
    refs = refs[1:]
    if latent:
        (qg_ref, kg_ref, q_ref, k_ref, v_ref, kc_ref, vc_ref,
         cq_ref, saq_ref, sbq_ref, ck_ref, sak_ref, sbk_ref, o_ref) = refs
    else:
        qg_ref, kg_ref, q_ref, k_ref, v_ref, o_ref, kn_ref = refs
    half = HEAD_DIM // 2
    kn = _head_rms(k_ref[...], kg_ref[...])
    if latent:
        kn = _apply_rope(kn, ck_ref[...], sak_ref[...], sbk_ref[...], half)
    else:
        kn_ref[...] = kn
    q = q_ref[...]
    tq = q.shape[0]
    heads = []
    for j in range(GQA_GROUP):
        qj = _head_rms(q[:, j * HEAD_DIM:(j + 1) * HEAD_DIM], qg_ref[...])
        if latent:
            qj = _apply_rope(qj, cq_ref[...], saq_ref[...], sbq_ref[...], half)
        heads.append(qj.astype(BF16))
    qs = jnp.concatenate(heads, axis=0)
    scale = HEAD_DIM ** -0.5
    scores, values = [], []
    if latent:
        scores.append(lax.dot_general(qs, kc_ref[...].astype(BF16), _NT, preferred_element_type=F32) * scale)
        values.append(vc_ref[...].astype(BF16))
    scores.append(lax.dot_general(qs, kn.astype(BF16), _NT, preferred_element_type=F32) * scale)
    values.append(v_ref[...].astype(BF16))
    o = _softmax_pv(scores, values)
    for j in range(GQA_GROUP):
        o_ref[:, j * HEAD_DIM:(j + 1) * HEAD_DIM] = o[j * tq:(j + 1) * tq].astype(o_ref.dtype)


def _gqa_attention(mix, proj, q_g, k_g, layer, cache_k, cache_v, tabs128):
    qg = q_g.reshape(1, HEAD_DIM)
    kg = k_g.reshape(1, HEAD_DIM)
    gw = GQA_GROUP * HEAD_DIM
    small = [pl.BlockSpec((1, HEAD_DIM), lambda *a: (0, 0)), pl.BlockSpec((1, HEAD_DIM), lambda *a: (0, 0))]
    qcol = COL_QB * LANES // gw
    mcol = DIFF_HEADS * HEAD_DIM // gw
    mix, kn = pl.pallas_call(
        functools.partial(_gqa_kernel, latent=False),
        grid=(BATCH, GQA_KV_HEADS),
        in_specs=[_ANY] + small + [
            pl.BlockSpec((SEQ, gw), lambda b, g: (b, qcol + g)),
            pl.BlockSpec((SEQ, LANES), lambda b, g: (b, COL_KB + g)),
            pl.BlockSpec((SEQ, LANES), lambda b, g: (b, COL_VB + g)),
        ],
        out_specs=[pl.BlockSpec((SEQ, gw), lambda b, g: (b, mcol + g)),
                   pl.BlockSpec((SEQ, LANES), lambda b, g: (b, g))],
        out_shape=[jax.ShapeDtypeStruct((N_TOK, MIX_WIDTH), BF16),
                   jax.ShapeDtypeStruct((N_CTX, GQA_KV_HEADS * HEAD_DIM), F32)],
        input_output_aliases={0: 0},
        compiler_params=_params("arbitrary", "arbitrary"),
        name="gqa_ctx",
    )(mix, qg, kg, proj, proj, proj)

    nq = DEC_SEQ // ATT_TQ
    q_row = lambda b, qi: (N_CTX // ATT_TQ) + b * nq + qi
    kv_row = lambda b: (N_CTX // DEC_SEQ) + b
    ck = cache_k.reshape(DEC_BATCH, DEPTH, PAST_LEN, GQA_KV_HEADS * HEAD_DIM)
    cv = cache_v.reshape(DEC_BATCH, DEPTH, PAST_LEN, GQA_KV_HEADS * HEAD_DIM)
    tab_q = pl.BlockSpec((ATT_TQ, LANES), lambda b, g, qi: (qi, 0))
    tab_k = pl.BlockSpec((DEC_SEQ, LANES), lambda b, g, qi: (0, 0))
    mix = pl.pallas_call(
        functools.partial(_gqa_kernel, latent=True),
        grid=(DEC_BATCH, GQA_KV_HEADS, nq),
        in_specs=[_ANY] + small + [
            pl.BlockSpec((ATT_TQ, gw), lambda b, g, qi: (q_row(b, qi), qcol + g)),
            pl.BlockSpec((DEC_SEQ, LANES), lambda b, g, qi: (kv_row(b), COL_KB + g)),
            pl.BlockSpec((DEC_SEQ, LANES), lambda b, g, qi: (kv_row(b), COL_VB + g)),
            pl.BlockSpec((None, None, PAST_LEN, LANES), lambda b, g, qi: (b, layer, 0, g)),
            pl.BlockSpec((None, None, PAST_LEN, LANES), lambda b, g, qi: (b, layer, 0, g)),
            tab_q, tab_q, tab_q, tab_k, tab_k, tab_k,
        ],
        out_specs=pl.BlockSpec((ATT_TQ, gw), lambda b, g, qi: (q_row(b, qi), mcol + g)),
        out_shape=jax.ShapeDtypeStruct((N_TOK, MIX_WIDTH), BF16),
        input_output_aliases={0: 0},
        compiler_params=_params("arbitrary", "arbitrary", "arbitrary"),
        name="gqa_lat",
    )(mix, qg, kg, proj, proj, proj, ck, cv, *tabs128, *tabs128)
    return mix, kn


def _ret_kernel(*refs, seq, latent):
    refs = refs[1:]
    if latent:
        (lgf_ref, lgb_ref, q_ref, k_ref, v_ref, gc_ref, s0f_ref, s0b_ref,
         cq_ref, saq_ref, sbq_ref, ck_ref, sak_ref, sbk_ref, o_ref) = refs
    else:
        lgf_ref, lgb_ref, q_ref, k_ref, v_ref, gc_ref, o_ref, sf_ref, sb_ref = refs
    hp = pl.program_id(1)
    qi = pl.program_id(2)
    q = q_ref[...]
    k = k_ref[...] * (RET_QK_DIM ** -0.5)
    if latent:
        half = RET_QK_DIM // 2
        q = _apply_rope(q, cq_ref[...], saq_ref[...], sbq_ref[...], half)
        k = _apply_rope(k, ck_ref[...], sak_ref[...], sbk_ref[...], half)
    tq = q.shape[0]
    t_idx = qi * tq + lax.broadcasted_iota(jnp.int32, (tq, seq), 0)
    s_idx = lax.broadcasted_iota(jnp.int32, (tq, seq), 1)
    rel = (t_idx - s_idx).astype(F32)
    t_col = (qi * tq + lax.broadcasted_iota(jnp.int32, (tq, 1), 0)).astype(F32)
    s_col = lax.broadcasted_iota(jnp.int32, (seq, 1), 0).astype(F32)
    lane_q = lax.broadcasted_iota(jnp.int32, q.shape, 1)
    lane_k = lax.broadcasted_iota(jnp.int32, k.shape, 1)
    kb = k.astype(BF16)
    for j in range(2):
        lgf = lgf_ref[2 * hp + j]
        lgb = lgb_ref[2 * hp + j]
        decay = (jnp.where(rel >= 0, jnp.exp(jnp.maximum(rel, 0.0) * lgf), 0.0)
                 + jnp.where(rel <= 0, jnp.exp(jnp.maximum(-rel, 0.0) * lgb), 0.0))
        in_head_q = (lane_q >= j * RET_QK_DIM) & (lane_q < (j + 1) * RET_QK_DIM)
        qj = jnp.where(in_head_q, q, 0.0).astype(BF16)
        vj = v_ref[:, j * HEAD_DIM:(j + 1) * HEAD_DIM].astype(BF16)
        scores = lax.dot_general(qj, kb, _NT, preferred_element_type=F32) * decay
        o = jnp.dot(scores.astype(BF16), vj, preferred_element_type=F32)
        if latent:
            o = o + (jnp.dot(qj, s0f_ref[...].astype(BF16), preferred_element_type=F32)
                     * jnp.exp((t_col + 1.0) * lgf))
            o = o + (jnp.dot(qj, s0b_ref[...].astype(BF16), preferred_element_type=F32)
                     * jnp.exp((seq - t_col) * lgb))
        else:
            in_head_k = (lane_k >= j * RET_QK_DIM) & (lane_k < (j + 1) * RET_QK_DIM)
            kj = jnp.where(in_head_k, k, 0.0)
            kf = (kj * jnp.exp((seq - 1.0 - s_col) * lgf)).astype(BF16)
            kr = (kj * jnp.exp(s_col * lgb)).astype(BF16)
            sf = lax.dot_general(kf, vj, _TN, preferred_element_type=F32)
            sb = lax.dot_general(kr, vj, _TN, preferred_element_type=F32)
            sf_ref[0, j] = sf[j * RET_QK_DIM:(j + 1) * RET_QK_DIM]
            sb_ref[0, j] = sb[j * RET_QK_DIM:(j + 1) * RET_QK_DIM]
        mu = jnp.mean(o, axis=-1, keepdims=True)
        oc = o - mu
        var = jnp.mean(oc * oc, axis=-1, keepdims=True)
        gcj = gc_ref[:, j * HEAD_DIM:(j + 1) * HEAD_DIM]
        silu = gcj * (1.0 / (1.0 + jnp.exp(-gcj)))
        o_ref[:, j * HEAD_DIM:(j + 1) * HEAD_DIM] = (oc * lax.rsqrt(var + NORM_EPS) * silu).astype(o_ref.dtype)


def _retention(mix, proj, lg_f, lg_b, layer, state_f, state_b, tabs64):
    pairs = RET_HEADS // 2
    pw = 2 * HEAD_DIM
    smem = pl.BlockSpec(memory_space=pltpu.SMEM)
    vcol = COL_VC * LANES // pw
    gcol = COL_GC * LANES // pw
    mcol = (DIFF_HEADS + GQA_Q_HEADS) * HEAD_DIM // pw
    mix, sf, sb = pl.pallas_call(
        functools.partial(_ret_kernel, seq=SEQ, latent=False),
        grid=(BATCH, pairs, 1),
        in_specs=[
            _ANY, smem, smem,
            pl.BlockSpec((SEQ, LANES), lambda b, p, qi: (b, COL_QC + p)),
            pl.BlockSpec((SEQ, LANES), lambda b, p, qi: (b, COL_KC + p)),
            pl.BlockSpec((SEQ, pw), lambda b, p, qi: (b, vcol + p)),
            pl.BlockSpec((SEQ, pw), lambda b, p, qi: (b, gcol + p)),
        ],
        out_specs=[
            pl.BlockSpec((SEQ, pw), lambda b, p, qi: (b, mcol + p)),
            pl.BlockSpec((1, 2, RET_QK_DIM, HEAD_DIM), lambda b, p, qi: (b, p, 0, 0)),
            pl.BlockSpec((1, 2, RET_QK_DIM, HEAD_DIM), lambda b, p, qi: (b, p, 0, 0)),
        ],
        out_shape=[
            jax.ShapeDtypeStruct((N_TOK, MIX_WIDTH), BF16),
            jax.ShapeDtypeStruct((BATCH, RET_HEADS, RET_QK_DIM, HEAD_DIM), F32),
            jax.ShapeDtypeStruct((BATCH, RET_HEADS, RET_QK_DIM, HEAD_DIM), F32),
        ],
        input_output_aliases={0: 0},
        compiler_params=_params("arbitrary", "arbitrary", "arbitrary"),
        name="ret_ctx",
    )(mix, lg_f, lg_b, proj, proj, proj, proj)

    nq = DEC_SEQ // ATT_TQ
    q_row = lambda b, qi: (N_CTX // ATT_TQ) + b * nq + qi
    kv_row = lambda b: (N_CTX // DEC_SEQ) + b
    s0f = state_f.reshape(DEC_BATCH, DEPTH, pairs, 2 * RET_QK_DIM, HEAD_DIM)
    s0b = state_b.reshape(DEC_BATCH, DEPTH, pairs, 2 * RET_QK_DIM, HEAD_DIM)
    tab_q = pl.BlockSpec((ATT_TQ, LANES), lambda b, p, qi: (qi, 0))
    tab_k = pl.BlockSpec((DEC_SEQ, LANES), lambda b, p, qi: (0, 0))
    state_spec = pl.BlockSpec((None, None, None, 2 * RET_QK_DIM, HEAD_DIM), lambda b, p, qi: (b, layer, p, 0, 0))
    mix = pl.pallas_call(
        functools.partial(_ret_kernel, seq=DEC_SEQ, latent=True),
        grid=(DEC_BATCH, pairs, nq),
        in_specs=[
            _ANY, smem, smem,
            pl.BlockSpec((ATT_TQ, LANES), lambda b, p, qi: (q_row(b, qi), COL_QC + p)),
            pl.BlockSpec((DEC_SEQ, LANES), lambda b, p, qi: (kv_row(b), COL_KC + p)),
            pl.BlockSpec((DEC_SEQ, pw), lambda b, p, qi: (kv_row(b), vcol + p)),
            pl.BlockSpec((ATT_TQ, pw), lambda b, p, qi: (q_row(b, qi), gcol + p)),
            state_spec, state_spec,
            tab_q, tab_q, tab_q, tab_k, tab_k, tab_k,
        ],
        out_specs=pl.BlockSpec((ATT_TQ, pw), lambda b, p, qi: (q_row(b, qi), mcol + p)),
        out_shape=jax.ShapeDtypeStruct((N_TOK, MIX_WIDTH), BF16),
        input_output_aliases={0: 0},
        compiler_params=_params("arbitrary", "arbitrary", "arbitrary"),
        name="ret_lat",
    )(mix, lg_f, lg_b, proj, proj, proj, proj, s0f, s0b, *tabs64, *tabs64)
    return mix, sf, sb


def _router_kernel(x_ref, g_ref, sh_ref, sc_ref, rw_ref, rb_ref, h_ref, e_ref, gate_ref):
    h = _ada_norm_value(x_ref[...], g_ref[...], sh_ref[0], sc_ref[0])
    h_ref[...] = h
    hh = h.astype(BF16)
    hl = (h - hh.astype(F32)).astype(BF16)
    w = rw_ref[...]
    wh = w.astype(BF16)
    wl = (w - wh.astype(F32)).astype(BF16)
    logits = (jnp.dot(hh, wh, preferred_element_type=F32) + jnp.dot(hh, wl, preferred_element_type=F32)
              + jnp.dot(hl, wh, preferred_element_type=F32) + rb_ref[...])
    lane = lax.broadcasted_iota(jnp.int32, logits.shape, 1).astype(F32)
    wide = lax.broadcasted_iota(jnp.int32, e_ref.shape, 1)
    e_out = jnp.zeros(e_ref.shape, F32)
    v_out = jnp.zeros(e_ref.shape, F32)
    top0 = None
    den = None
    for kk in range(TOP_K):
        m = jnp.max(logits, axis=-1, keepdims=True)
        idx = jnp.min(jnp.where(logits == m, lane, float(N_EXPERTS)), axis=-1, keepdims=True)
        logits = jnp.where(lane == idx, -jnp.inf, logits)
        if kk == 0:
            top0 = m
        ex = jnp.exp(m - top0)
        den = ex if den is None else den + ex
        e_out = jnp.where(wide == kk, idx, e_out)
        v_out = jnp.where(wide == kk, ex, v_out)
    e_ref[...] = e_out.astype(jnp.int32)
    gate_ref[...] = v_out / den


def _router(x, g, modt, layer, router_w, router_b):
    tm = 256
    sh0 = _mod_index(layer, 3)
    sc0 = _mod_index(layer, 4)
    return pl.pallas_call(
        _router_kernel,
        grid=(N_TOK // tm,),
        in_specs=[
            pl.BlockSpec((tm, D_MODEL), lambda i: (i, 0)),
            pl.BlockSpec((1, D_MODEL), lambda i: (0, 0)),
            pl.BlockSpec((1, 1, D_MODEL), lambda i: (sh0 + _stream_of_row_tile(i, tm), 0, 0)),
            pl.BlockSpec((1, 1, D_MODEL), lambda i: (sc0 + _stream_of_row_tile(i, tm), 0, 0)),
            pl.BlockSpec((None, D_MODEL, N_EXPERTS), lambda i: (layer, 0, 0)),
            pl.BlockSpec((None, 1, N_EXPERTS), lambda i: (layer, 0, 0)),
        ],
        out_specs=[
            pl.BlockSpec((tm, D_MODEL), lambda i: (i, 0)),
            pl.BlockSpec((tm, LANES), lambda i: (i, 0)),
            pl.BlockSpec((tm, LANES), lambda i: (i, 0)),
        ],
        out_shape=[
            jax.ShapeDtypeStruct((N_TOK, D_MODEL), F32),
            jax.ShapeDtypeStruct((N_TOK, LANES), jnp.int32),
            jax.ShapeDtypeStruct((N_TOK, LANES), F32),
        ],
        compiler_params=_params("arbitrary"),
        name="router",
    )(x, g.reshape(1, D_MODEL), modt, modt, router_w, router_b.reshape(DEPTH, 1, N_EXPERTS))


def _route_meta(top_e):
    n_assign = N_TOK * TOP_K
    n_items = -(-n_assign // MOE_ROWS) + N_EXPERTS
    flat_e = top_e.reshape(n_assign)
    order = jnp.argsort(flat_e).astype(jnp.int32)
    inv = jnp.argsort(order).astype(jnp.int32)
    onehot = (flat_e[:, None] == jnp.arange(N_EXPERTS, dtype=jnp.int32)[None, :]).astype(jnp.int32)
    counts = jnp.sum(onehot, axis=0)
    padded = (counts + MOE_ROWS - 1) // MOE_ROWS * MOE_ROWS
    start = jnp.cumsum(counts) - counts
    pad_end = jnp.cumsum(padded)
    pad_start = pad_end - padded
    slot_of = inv + jnp.sum(onehot * (pad_start - start)[None, :], axis=1)
    tok_sorted = jnp.concatenate([order // TOP_K, jnp.zeros((8,), jnp.int32)])
    item_start = jnp.arange(n_items, dtype=jnp.int32) * MOE_ROWS
    item_e = jnp.minimum(jnp.searchsorted(pad_end, item_start, side='right'), N_EXPERTS - 1).astype(jnp.int32)
    off = item_start - pad_start[item_e]
    item_rows = jnp.clip(counts[item_e] - off, 0, MOE_ROWS).astype(jnp.int32)
    item_rows8 = (item_rows + 7) // 8 * 8
    item_src = jnp.clip(start[item_e] + off, 0, n_assign - 1).astype(jnp.int32)
    n_act = (pad_end[-1] // MOE_ROWS).astype(jnp.int32).reshape(1)
    return tok_sorted, slot_of.astype(jnp.int32), (item_e, item_rows, item_rows8, item_src, n_act), n_items


def _expert_kernel(ie_ref, rows_ref, rows8_ref, src_ref, nact_ref, tok_ref,
                   h_hbm, wg_ref, wu_ref, bg_ref, bu_ref, wd_ref, bd_ref, o_ref, x_ref, act_ref, sem):
    it = pl.program_id(0)
    s = pl.program_id(1)
    n_act = nact_ref[0]
    active = it < n_act
    rows = rows_ref[it]
    n_full = rows // MOE_SUB
    rem = rows - n_full * MOE_SUB
    n_main = n_full + (rem > MOE_SUB // 2).astype(jnp.int32)
    tail0 = pl.multiple_of(n_full * MOE_SUB, MOE_SUB)
    short_tail = (rem > 0) & (rem <= MOE_SUB // 4)
    half_tail = (rem > MOE_SUB // 4) & (rem <= MOE_SUB // 2)

    def row_copy(tok, r):
        return pltpu.make_async_copy(h_hbm.at[pl.ds(tok, 1), :], x_ref.at[pl.ds(r, 1), :], sem.at[0])

    def issue(item, lo, hi):
        base = src_ref[item]

        def body(g, carry):
            r0 = pl.multiple_of(g * 8, 8)
            for u in range(8):
                row_copy(tok_ref[base + r0 + u], r0 + u).start()
            return carry
        lax.fori_loop(lo // 8, hi // 8, body, 0)

    @pl.when(active & (s == 0))
    def _():
        @pl.when(it == 0)
        def _():
            x_ref[...] = jnp.zeros(x_ref.shape, F32)
            issue(0, 0, rows8_ref[0])

        def wait_body(g, carry):
            for u in range(8):
                row_copy(0, 0).wait()
            return carry
        lax.fori_loop(0, rows8_ref[it] // 8, wait_body, 0)

    def gate_up(r0, m):
        xs = x_ref[pl.ds(r0, m), :].astype(BF16)
        gate = jnp.dot(xs, wg_ref[...].astype(BF16), preferred_element_type=F32) + bg_ref[...]
        up = jnp.dot(xs, wu_ref[...].astype(BF16), preferred_element_type=F32) + bu_ref[...]
        gate = jnp.minimum(gate, SWIGLU_LIMIT)
        up = jnp.clip(up, -SWIGLU_LIMIT, SWIGLU_LIMIT)
        a = (up + 1.0) * gate * (1.0 / (1.0 + jnp.exp(-SWIGLU_ALPHA * gate)))
        act_ref[s, pl.ds(r0, m), :] = a.astype(BF16)

    def down(r0, m):
        acc = None
        for f in range(MOE_NF):
            d = jnp.dot(act_ref[f, pl.ds(r0, m), :], wd_ref[f * MOE_TF:(f + 1) * MOE_TF, :].astype(BF16),
                        preferred_element_type=F32)
            acc = d if acc is None else acc + d
        o_ref[pl.ds(r0, m), :] = acc + bd_ref[...]

    def run_rows(fn):
        def body(i, carry):
            fn(pl.multiple_of(i * 2 * MOE_SUB, 2 * MOE_SUB), 2 * MOE_SUB)
            return carry
        lax.fori_loop(0, n_main // 2, body, 0)

        @pl.when(n_main % 2 == 1)
        def _():
            fn(pl.multiple_of((n_main - 1) * MOE_SUB, MOE_SUB), MOE_SUB)

        @pl.when(short_tail)
        def _():
            fn(tail0, MOE_SUB // 4)

        @pl.when(half_tail)
        def _():
            fn(tail0, MOE_SUB // 2)

    @pl.when(active & (s < MOE_NF))
    def _():
        run_rows(gate_up)

    @pl.when(active & (s >= MOE_NF))
    def _():
        @pl.when(it + 1 < n_act)
        def _():
            d = s - MOE_NF
            issue(it + 1, d * MOE_ISSUE, jnp.minimum((d + 1) * MOE_ISSUE, rows8_ref[it + 1]))
        run_rows(down)


def _experts(h, tok_sorted, item_meta, n_items, layer, w_gu, b_gu, w_dn, b_dn):
    n_steps = MOE_NF + MOE_ND
    cap = n_items * MOE_ROWS

    def pos(it, s, ie, nact):
        last = jnp.maximum(nact[0] - 1, 0)
        itc = jnp.minimum(it, last)
        sc = jnp.where(it < nact[0], s, n_steps - 1)
        return itc, sc, ie[itc]

    def wg_map(it, s, ie, rows, rows8, src, nact, tok):
        _, sc, e = pos(it, s, ie, nact)
        return (layer, e, 0, jnp.minimum(sc, MOE_NF - 1))

    def wu_map(it, s, ie, rows, rows8, src, nact, tok):
        _, sc, e = pos(it, s, ie, nact)
        return (layer, e, 0, MOE_NF + jnp.minimum(sc, MOE_NF - 1))

    def wd_map(it, s, ie, rows, rows8, src, nact, tok):
        _, sc, e = pos(it, s, ie, nact)
        return (layer, e, 0, jnp.clip(sc - MOE_NF, 0, MOE_ND - 1))

    def o_map(it, s, ie, rows, rows8, src, nact, tok):
        itc, sc, _ = pos(it, s, ie, nact)
        return (itc, jnp.clip(sc - MOE_NF, 0, MOE_ND - 1))

    grid_spec = pltpu.PrefetchScalarGridSpec(
        num_scalar_prefetch=6,
        grid=(n_items, n_steps),
        in_specs=[
            _ANY,
            pl.BlockSpec((None, None, D_MODEL, MOE_TF), wg_map),
            pl.BlockSpec((None, None, D_MODEL, MOE_TF), wu_map),
            pl.BlockSpec((None, None, 1, MOE_TF), wg_map),
            pl.BlockSpec((None, None, 1, MOE_TF), wu_map),
            pl.BlockSpec((None, None, EXPERT_FF, MOE_TD), wd_map),
            pl.BlockSpec((None, None, 1, MOE_TD), wd_map),
        ],
        out_specs=pl.BlockSpec((MOE_ROWS, MOE_TD), o_map),
        scratch_shapes=[
            pltpu.VMEM((MOE_ROWS, D_MODEL), F32),
            pltpu.VMEM((MOE_NF, MOE_ROWS, MOE_TF), BF16),
            pltpu.SemaphoreType.DMA((1,)),
        ],
    )
    b_gu4 = b_gu.reshape(DEPTH, N_EXPERTS, 1, 2 * EXPERT_FF)
    return pl.pallas_call(
        _expert_kernel,
        grid_spec=grid_spec,
        out_shape=jax.ShapeDtypeStruct((cap, D_MODEL), F32),
        compiler_params=_params("arbitrary", "arbitrary"),
        name="moe_experts",
    )(*item_meta, tok_sorted, h, w_gu, w_gu, b_gu4, b_gu4, w_dn, b_dn.reshape(DEPTH, N_EXPERTS, 1, D_MODEL))


def _combine_kernel(slot_ref, y_hbm, x_ref, gates_ref, g2_ref, fg_ref, o_ref, buf, sem, *, final_norm):
    i = pl.program_id(0)
    n = pl.num_programs(0)

    def copy(tile, slot, r, kk):
        row = slot_ref[(tile * COMBINE_TOK + r) * TOP_K + kk]
        return pltpu.make_async_copy(y_hbm.at[pl.ds(row, 1), :], buf.at[slot, kk, pl.ds(r, 1), :], sem.at[slot])

    def issue(tile, slot):
        def body(r, carry):
            for kk in range(TOP_K):
                copy(tile, slot, r, kk).start()
            return carry
        lax.fori_loop(0, COMBINE_TOK, body, 0)

    @pl.when(i == 0)
    def _():
        issue(0, 0)

    @pl.when(i + 1 < n)
    def _():
        issue(i + 1, (i + 1) % 2)

    slot = i % 2

    def wait_body(r, carry):
        for kk in range(TOP_K):
            copy(i, slot, r, kk).wait()
        return carry
    lax.fori_loop(0, COMBINE_TOK, wait_body, 0)

    gates = gates_ref[...]
    y = gates[:, 0:1] * buf[slot, 0]
    for kk in range(1, TOP_K):
        y = y + gates[:, kk:kk + 1] * buf[slot, kk]
    out = x_ref[...] + g2_ref[0] * y
    if final_norm:
        ms = jnp.mean(out * out, axis=-1, keepdims=True)
        out = out * lax.rsqrt(ms + NORM_EPS) * fg_ref[...]
    o_ref[...] = out


def _combine(yb, slot_of, x, gates, modt, layer, final_g, final_norm):
    tc = COMBINE_TOK
    g0 = _mod_index(layer, 5)
    grid_spec = pltpu.PrefetchScalarGridSpec(
        num_scalar_prefetch=1,
        grid=(N_TOK // tc,),
        in_specs=[
            pl.BlockSpec(memory_space=pl.ANY),
            pl.BlockSpec((tc, D_MODEL), lambda i, sl: (i, 0)),
            pl.BlockSpec((tc, LANES), lambda i, sl: (i, 0)),
            pl.BlockSpec((1, 1, D_MODEL), lambda i, sl: (g0 + _stream_of_row_tile(i, tc), 0, 0)),
            pl.BlockSpec((1, D_MODEL), lambda i, sl: (0, 0)),
        ],
        out_specs=pl.BlockSpec((tc, D_MODEL), lambda i, sl: (i, 0)),
        scratch_shapes=[pltpu.VMEM((2, TOP_K, tc, D_MODEL), F32), pltpu.SemaphoreType.DMA((2,))],
    )
    return pl.pallas_call(
        functools.partial(_combine_kernel, final_norm=final_norm),
        grid_spec=grid_spec,
        out_shape=jax.ShapeDtypeStruct((N_TOK, D_MODEL), F32),
        compiler_params=_params("arbitrary"),
        name="moe_combine",
    )(slot_of, yb, x, gates, modt, final_g.reshape(1, D_MODEL))


def kernel(x_prompt, x_sample, cache_diff_k, cache_diff_v, cache_gqa_k, cache_gqa_v, state_ret_fwd, state_ret_bwd, c, c_ctx, w_mod, b_mod, norm_mix_g, norm_ffn_g, w_in, w_out, diff_lambda, diff_subln_g, gqa_q_norm_g, gqa_k_norm_g, ret_log_decay_fwd, ret_log_decay_bwd, router_w, router_b, moe_w_gate_up, moe_b_gate_up, moe_w_down, moe_b_down, final_norm_g):
    x = jnp.concatenate([x_prompt.reshape(N_CTX, D_MODEL), x_sample.reshape(N_LAT, D_MODEL)], axis=0)
    cond_raw = jnp.concatenate([c_ctx[None].astype(F32), c.astype(F32),
                                jnp.zeros((8 - N_STREAM, D_MODEL), F32)], axis=0)
    modt = _modulation(cond_raw, w_mod, b_mod)
    tabs64 = _rope_tables(DEC_SEQ, DIFF_QK_DIM)
    tabs128 = _rope_tables(DEC_SEQ, HEAD_DIM)

    new_dk, new_dv, new_gk, new_gv, new_sf, new_sb = [], [], [], [], [], []
    for layer in range(DEPTH):
        lam_init = 0.8 - 0.6 * math.exp(-0.3 * layer)
        h = _adanorm(x, norm_mix_g[layer], modt, layer, 0, 1)
        proj = _in_proj(h, w_in[layer])
        mix = _diff_attention(proj, diff_lambda[layer], diff_subln_g[layer], lam_init, layer,
                              cache_diff_k, cache_diff_v, tabs64)
        mix, kn = _gqa_attention(mix, proj, gqa_q_norm_g[layer], gqa_k_norm_g[layer], layer,
                                 cache_gqa_k, cache_gqa_v, tabs128)
        mix, sf, sb = _retention(mix, proj, ret_log_decay_fwd[layer], ret_log_decay_bwd[layer], layer,
                                 state_ret_fwd, state_ret_bwd, tabs64)
        x = _out_proj(mix, w_out[layer], x, modt, layer, 2)

        h2, top_e, gates = _router(x, norm_ffn_g[layer], modt, layer, router_w, router_b)
        tok_sorted, slot_of, item_meta, n_items = _route_meta(top_e[:, :TOP_K])
        yb = _experts(h2, tok_sorted, item_meta, n_items, layer,
                      moe_w_gate_up, moe_b_gate_up, moe_w_down, moe_b_down)
        x = _combine(yb, slot_of, x, gates, modt, layer, final_norm_g, layer == DEPTH - 1)

        pc = proj[:N_CTX]
        new_dk.append(pc[:, COL_KA * LANES:COL_VA * LANES].reshape(BATCH, SEQ, DIFF_HEADS, 2, DIFF_QK_DIM))
        new_dv.append(pc[:, COL_VA * LANES:COL_QB * LANES].reshape(BATCH, SEQ, DIFF_HEADS, HEAD_DIM))
        new_gk.append(kn.reshape(BATCH, SEQ, GQA_KV_HEADS, HEAD_DIM))
        new_gv.append(pc[:, COL_VB * LANES:COL_QC * LANES].reshape(BATCH, SEQ, GQA_KV_HEADS, HEAD_DIM))
        new_sf.append(sf)
        new_sb.append(sb)

    y_prompt = x[:N_CTX].reshape(BATCH, SEQ, D_MODEL)
    y_sample = x[N_CTX:].reshape(DEC_BATCH, DEC_SEQ, D_MODEL)
    return (y_prompt, y_sample, jnp.stack(new_dk, axis=1), jnp.stack(new_dv, axis=1),
            jnp.stack(new_gk, axis=1), jnp.stack(new_gv, axis=1),
            jnp.stack(new_sf, axis=1), jnp.stack(new_sb, axis=1))
```

```python
import functools
import math

import jax
import jax.numpy as jnp
from jax import lax
from jax.experimental import pallas as pl
from jax.experimental.pallas import tpu as pltpu

F32 = jnp.float32
BF16 = jnp.bfloat16

D_MODEL = 4096
BATCH = 16
SEQ = 256
DEPTH = 2
DEC_BATCH = 2
DEC_SEQ = 1024
PAST_LEN = 512
GRID_W = 64
HEAD_DIM = 128
DIFF_HEADS = 8
DIFF_QK_DIM = 64
GQA_Q_HEADS = 16
GQA_KV_HEADS = 4
GQA_GROUP = 4
RET_HEADS = 8
RET_QK_DIM = 64
N_EXPERTS = 32
TOP_K = 4
EXPERT_FF = D_MODEL // 2
SWIGLU_LIMIT = 7.0
SWIGLU_ALPHA = 1.702
ROPE_THETA = 10000.0
NORM_EPS = 1e-6
N_MOD = 6

N_CTX = BATCH * SEQ
N_LAT = DEC_BATCH * DEC_SEQ
N_TOK = N_CTX + N_LAT
N_STREAM = 1 + DEC_BATCH
IN_WIDTH = 9216
MIX_WIDTH = 4096

LANES = 128
COL_QA, COL_KA, COL_VA, COL_QB, COL_KB, COL_VB, COL_QC, COL_KC, COL_VC, COL_GC = 0, 8, 16, 24, 40, 44, 48, 52, 56, 64

ATT_TQ = 256
DIFF_HB = 2
DIFF_HB_CTX = 4
MOE_ROWS = 1024
MOE_SUB = 256
MOE_TF = 256
MOE_TD = 512
MOE_NF = EXPERT_FF // MOE_TF
MOE_ND = D_MODEL // MOE_TD
MOE_ISSUE = MOE_ROWS // MOE_ND
COMBINE_TOK = 128
VMEM_LIMIT = 60 * 1024 * 1024


def _params(*sem):
    return pltpu.CompilerParams(dimension_semantics=sem, vmem_limit_bytes=VMEM_LIMIT)


def _stream_of_row_tile(i, tm):
    r = i * tm
    return jnp.where(r < N_CTX, 0, 1 + (r - N_CTX) // DEC_SEQ)


def _mod_index(layer, which):
    return (layer * N_MOD + which) * N_STREAM


def _mod_kernel(c_ref, w_ref, b_ref, o_ref):
    c = c_ref[...]
    cond = c * (1.0 / (1.0 + jnp.exp(-c)))
    o_ref[0] = jnp.dot(cond.astype(BF16), w_ref[0].astype(BF16), preferred_element_type=F32) + b_ref[0]


def _modulation(cond_raw, w_mod, b_mod):
    tn = 1024
    n_out = N_MOD * D_MODEL
    out = pl.pallas_call(
        _mod_kernel,
        grid=(DEPTH, n_out // tn),
        in_specs=[
            pl.BlockSpec((8, D_MODEL), lambda l, j: (0, 0)),
            pl.BlockSpec((1, D_MODEL, tn), lambda l, j: (l, 0, j)),
            pl.BlockSpec((1, 1, tn), lambda l, j: (l, 0, j)),
        ],
        out_specs=pl.BlockSpec((1, 8, tn), lambda l, j: (l, 0, j)),
        out_shape=jax.ShapeDtypeStruct((DEPTH, 8, n_out), F32),
        compiler_params=_params("arbitrary", "arbitrary"),
        name="modulation",
    )(cond_raw, w_mod, b_mod.reshape(DEPTH, 1, n_out))
    out = out.reshape(DEPTH, 8, N_MOD, D_MODEL)[:, :N_STREAM]
    return out.transpose(0, 2, 1, 3).reshape(DEPTH * N_MOD * N_STREAM, 1, D_MODEL)


def _ada_norm_value(x, g, shift, scale):
    ms = jnp.mean(x * x, axis=-1, keepdims=True)
    y = x * lax.rsqrt(ms + NORM_EPS) * g
    return y * (1.0 + scale) + shift


def _adanorm_kernel(x_ref, g_ref, sh_ref, sc_ref, o_ref):
    o_ref[...] = _ada_norm_value(x_ref[...], g_ref[...], sh_ref[0], sc_ref[0]).astype(o_ref.dtype)


def _adanorm(x, g, modt, layer, which_shift, which_scale):
    tm = 256
    sh0 = _mod_index(layer, which_shift)
    sc0 = _mod_index(layer, which_scale)
    return pl.pallas_call(
        _adanorm_kernel,
        grid=(N_TOK // tm,),
        in_specs=[
            pl.BlockSpec((tm, D_MODEL), lambda i: (i, 0)),
            pl.BlockSpec((1, D_MODEL), lambda i: (0, 0)),
            pl.BlockSpec((1, 1, D_MODEL), lambda i: (sh0 + _stream_of_row_tile(i, tm), 0, 0)),
            pl.BlockSpec((1, 1, D_MODEL), lambda i: (sc0 + _stream_of_row_tile(i, tm), 0, 0)),
        ],
        out_specs=pl.BlockSpec((tm, D_MODEL), lambda i: (i, 0)),
        out_shape=jax.ShapeDtypeStruct((N_TOK, D_MODEL), BF16),
        compiler_params=_params("arbitrary"),
        name="adanorm",
    )(x, g.reshape(1, D_MODEL), modt, modt)


def _mm_kernel(x_ref, w_ref, o_ref, wb_ref):
    @pl.when(pl.program_id(1) == 0)
    def _():
        wb_ref[...] = w_ref[...].astype(BF16)

    o_ref[...] = jnp.dot(x_ref[...], wb_ref[...], preferred_element_type=F32)


def _mm_residual_kernel(x_ref, w_ref, res_ref, gate_ref, o_ref, wb_ref):
    @pl.when(pl.program_id(1) == 0)
    def _():
        wb_ref[...] = w_ref[...].astype(BF16)

    acc = jnp.dot(x_ref[...], wb_ref[...], preferred_element_type=F32)
    o_ref[...] = res_ref[...] + gate_ref[0] * acc


def _in_proj(h, w):
    tm, tn = 1024, 512
    k = h.shape[1]
    n = w.shape[1]
    return pl.pallas_call(
        _mm_kernel,
        grid=(n // tn, N_TOK // tm),
        in_specs=[
            pl.BlockSpec((tm, k), lambda j, i: (i, 0)),
            pl.BlockSpec((k, tn), lambda j, i: (0, j)),
        ],
        out_specs=pl.BlockSpec((tm, tn), lambda j, i: (i, j)),
        out_shape=jax.ShapeDtypeStruct((N_TOK, n), F32),
        scratch_shapes=[pltpu.VMEM((k, tn), BF16)],
        compiler_params=_params("arbitrary", "arbitrary"),
        name="in_proj",
    )(h, w)


def _out_proj(mix, w, res, modt, layer, which_gate):
    tm, tn = 1024, 512
    k = mix.shape[1]
    n = w.shape[1]
    g0 = _mod_index(layer, which_gate)
    return pl.pallas_call(
        _mm_residual_kernel,
        grid=(n // tn, N_TOK // tm),
        in_specs=[
            pl.BlockSpec((tm, k), lambda j, i: (i, 0)),
            pl.BlockSpec((k, tn), lambda j, i: (0, j)),
            pl.BlockSpec((tm, tn), lambda j, i: (i, j)),
            pl.BlockSpec((1, 1, tn), lambda j, i: (g0 + _stream_of_row_tile(i, tm), 0, j)),
        ],
        out_specs=pl.BlockSpec((tm, tn), lambda j, i: (i, j)),
        out_shape=jax.ShapeDtypeStruct((N_TOK, n), F32),
        scratch_shapes=[pltpu.VMEM((k, tn), BF16)],
        compiler_params=_params("arbitrary", "arbitrary"),
        name="out_proj",
    )(mix, w, res, modt)


def _rope_tables(n_tokens, unit):
    rows = n_tokens // GRID_W
    row = jnp.repeat(jnp.arange(rows), GRID_W).astype(F32)
    col = jnp.tile(jnp.arange(GRID_W), rows).astype(F32)
    n_freq = unit // 4
    inv_freq = jnp.power(ROPE_THETA, -jnp.arange(n_freq, dtype=F32) / n_freq)
    ang = jnp.concatenate([row[:, None] * inv_freq, col[:, None] * inv_freq], axis=-1)
    cos, sin = jnp.cos(ang), jnp.sin(ang)
    zero = jnp.zeros_like(sin)
    reps = LANES // unit
    c = jnp.tile(jnp.concatenate([cos, cos], axis=-1), (1, reps))
    sa = jnp.tile(jnp.concatenate([-sin, zero], axis=-1), (1, reps))
    sb = jnp.tile(jnp.concatenate([zero, sin], axis=-1), (1, reps))
    return c, sa, sb


def _apply_rope(x, c, sa, sb, half):
    return x * c + pltpu.roll(x, LANES - half, 1) * sa + pltpu.roll(x, half, 1) * sb


_ANY = pl.BlockSpec(memory_space=pl.ANY)
_NT = (((1,), (1,)), ((), ()))
_TN = (((0,), (0,)), ((), ()))


def _softmax_pv(score_list, value_list):
    m = score_list[0].max(axis=-1, keepdims=True)
    for s in score_list[1:]:
        m = jnp.maximum(m, s.max(axis=-1, keepdims=True))
    acc = None
    den = None
    for s, v in zip(score_list, value_list):
        p = jnp.exp(s - m)
        d = p.sum(axis=-1, keepdims=True)
        o = jnp.dot(p.astype(BF16), v, preferred_element_type=F32)
        acc = o if acc is None else acc + o
        den = d if den is None else den + d
    return acc / den


def _diff_kernel(*refs, lam_init, latent, hb):
    if latent:
        refs = refs[1:]
        (lamv_ref, g_ref, q_ref, k_ref, v_ref, kc_ref, vc_ref,
         cq_ref, saq_ref, sbq_ref, ck_ref, sak_ref, sbk_ref, o_ref) = refs
    else:
        lamv_ref, g_ref, q_ref, k_ref, v_ref, o_ref = refs
    lv = lamv_ref[...]
    lam = (jnp.exp(jnp.sum(lv[0:1] * lv[1:2], axis=-1, keepdims=True))
           - jnp.exp(jnp.sum(lv[2:3] * lv[3:4], axis=-1, keepdims=True)) + lam_init)
    scale = DIFF_QK_DIM ** -0.5
    half = DIFF_QK_DIM // 2
    for hh in range(hb):
        cols = slice(hh * LANES, (hh + 1) * LANES)
        q = q_ref[:, cols]
        k = k_ref[:, cols]
        if latent:
            q = _apply_rope(q, cq_ref[...], saq_ref[...], sbq_ref[...], half)
            k = _apply_rope(k, ck_ref[...], sak_ref[...], sbk_ref[...], half)
        lane = lax.broadcasted_iota(jnp.int32, q.shape, 1)
        first = lane < DIFF_QK_DIM
        kb = k.astype(BF16)
        vb = v_ref[:, cols].astype(BF16)
        if latent:
            kcb = kc_ref[:, cols].astype(BF16)
            vcb = vc_ref[:, cols].astype(BF16)
        outs = []
        for m in range(2):
            qm = jnp.where(first if m == 0 else jnp.logical_not(first), q, 0.0).astype(BF16)
            scores, values = [], []
            if latent:
                scores.append(lax.dot_general(qm, kcb, _NT, preferred_element_type=F32) * scale)
                values.append(vcb)
            scores.append(lax.dot_general(qm, kb, _NT, preferred_element_type=F32) * scale)
            values.append(vb)
            outs.append(_softmax_pv(scores, values))
        o = outs[0] - lam * outs[1]
        ms = jnp.mean(o * o, axis=-1, keepdims=True)
        o = o * lax.rsqrt(ms + NORM_EPS) * g_ref[...] * (1.0 - lam_init)
        o_ref[:, cols] = o.astype(o_ref.dtype)


def _diff_attention(proj, lamv, subln_g, lam_init, layer, cache_k, cache_v, tabs64):
    g = subln_g.reshape(1, HEAD_DIM)
    hw = DIFF_HB * LANES
    small = [pl.BlockSpec((4, DIFF_QK_DIM), lambda *a: (0, 0)), pl.BlockSpec((1, HEAD_DIM), lambda *a: (0, 0))]
    hwc = DIFF_HB_CTX * LANES
    ctx = pl.pallas_call(
        functools.partial(_diff_kernel, lam_init=lam_init, latent=False, hb=DIFF_HB_CTX),
        grid=(BATCH, DIFF_HEADS // DIFF_HB_CTX),
        in_specs=small + [
            pl.BlockSpec((SEQ, hwc), lambda b, h: (b, COL_QA // DIFF_HB_CTX + h)),
            pl.BlockSpec((SEQ, hwc), lambda b, h: (b, COL_KA // DIFF_HB_CTX + h)),
            pl.BlockSpec((SEQ, hwc), lambda b, h: (b, COL_VA // DIFF_HB_CTX + h)),
        ],
        out_specs=pl.BlockSpec((SEQ, hwc), lambda b, h: (b, h)),
        out_shape=jax.ShapeDtypeStruct((N_TOK, MIX_WIDTH), BF16),
        compiler_params=_params("arbitrary", "arbitrary"),
        name="diff_ctx",
    )(lamv, g, proj, proj, proj)

    nq = DEC_SEQ // ATT_TQ
    q_row = lambda b, qi: (N_CTX // ATT_TQ) + b * nq + qi
    kv_row = lambda b: (N_CTX // DEC_SEQ) + b
    ck = cache_k.reshape(DEC_BATCH, DEPTH, PAST_LEN, DIFF_HEADS * HEAD_DIM)
    cv = cache_v.reshape(DEC_BATCH, DEPTH, PAST_LEN, DIFF_HEADS * HEAD_DIM)
    tab_q = pl.BlockSpec((ATT_TQ, LANES), lambda b, h, qi: (qi, 0))
    tab_k = pl.BlockSpec((DEC_SEQ, LANES), lambda b, h, qi: (0, 0))
    return pl.pallas_call(
        functools.partial(_diff_kernel, lam_init=lam_init, latent=True, hb=DIFF_HB),
        grid=(DEC_BATCH, DIFF_HEADS // DIFF_HB, nq),
        in_specs=[_ANY] + small + [
            pl.BlockSpec((ATT_TQ, hw), lambda b, h, qi: (q_row(b, qi), COL_QA // DIFF_HB + h)),
            pl.BlockSpec((DEC_SEQ, hw), lambda b, h, qi: (kv_row(b), COL_KA // DIFF_HB + h)),
            pl.BlockSpec((DEC_SEQ, hw), lambda b, h, qi: (kv_row(b), COL_VA // DIFF_HB + h)),
            pl.BlockSpec((None, None, PAST_LEN, hw), lambda b, h, qi: (b, layer, 0, h)),
            pl.BlockSpec((None, None, PAST_LEN, hw), lambda b, h, qi: (b, layer, 0, h)),
            tab_q, tab_q, tab_q, tab_k, tab_k, tab_k,
        ],
        out_specs=pl.BlockSpec((ATT_TQ, hw), lambda b, h, qi: (q_row(b, qi), h)),
        out_shape=jax.ShapeDtypeStruct((N_TOK, MIX_WIDTH), BF16),
        input_output_aliases={0: 0},
        compiler_params=_params("arbitrary", "arbitrary", "arbitrary"),
        name="diff_lat",
    )(ctx, lamv, g, proj, proj, proj, ck, cv, *tabs64, *tabs64)


def _head_rms(x, g):
    ms = jnp.mean(x * x, axis=-1, keepdims=True)
    return x * lax.rsqrt(ms + NORM_EPS) * g


def _gqa_kernel(*refs, latent):
    refs = refs[1:]
    if latent:
        (qg_ref, kg_ref, q_ref, k_ref, v_ref, kc_ref, vc_ref,
         cq_ref, saq_ref, sbq_ref, ck_ref, sak_ref, sbk_ref, o_ref) = refs
    else:
        qg_ref, kg_ref, q_ref, k_ref, v_ref, o_ref, kn_ref = refs
    half = HEAD_DIM // 2
    kn = _head_rms(k_ref[...], kg_ref[...])
    if latent:
        kn = _apply_rope(kn, ck_ref[...], sak_ref[...], sbk_ref[...], half)
    else:
        kn_ref[...] = kn
    q = q_ref[...]
    tq = q.shape[0]
    heads = []
    for j in range(GQA_GROUP):
        qj = _head_rms(q[:, j * HEAD_DIM:(j + 1) * HEAD_DIM], qg_ref[...])
        if latent:
            qj = _apply_rope(qj, cq_ref[...], saq_ref[...], sbq_ref[...], half)
        heads.append(qj.astype(BF16))
    qs = jnp.concatenate(heads, axis=0)
    scale = HEAD_DIM ** -0.5
    scores, values = [], []
    if latent:
        scores.append(lax.dot_general(qs, kc_ref[...].astype(BF16), _NT, preferred_element_type=F32) * scale)
        values.append(vc_ref[...].astype(BF16))
    scores.append(lax.dot_general(qs, kn.astype(BF16), _NT, preferred_element_type=F32) * scale)
    values.append(v_ref[...].astype(BF16))
    o = _softmax_pv(scores, values)
    for j in range(GQA_GROUP):
        o_ref[:, j * HEAD_DIM:(j + 1) * HEAD_DIM] = o[j * tq:(j + 1) * tq].astype(o_ref.dtype)


def _gqa_attention(mix, proj, q_g, k_g, layer, cache_k, cache_v, tabs128):
    qg = q_g.reshape(1, HEAD_DIM)
    kg = k_g.reshape(1, HEAD_DIM)
    gw = GQA_GROUP * HEAD_DIM
    small = [pl.BlockSpec((1, HEAD_DIM), lambda *a: (0, 0)), pl.BlockSpec((1, HEAD_DIM), lambda *a: (0, 0))]
    qcol = COL_QB * LANES // gw
    mcol = DIFF_HEADS * HEAD_DIM // gw
    mix, kn = pl.pallas_call(
        functools.partial(_gqa_kernel, latent=False),
        grid=(BATCH, GQA_KV_HEADS),
        in_specs=[_ANY] + small + [
            pl.BlockSpec((SEQ, gw), lambda b, g: (b, qcol + g)),
            pl.BlockSpec((SEQ, LANES), lambda b, g: (b, COL_KB + g)),
            pl.BlockSpec((SEQ, LANES), lambda b, g: (b, COL_VB + g)),
        ],
        out_specs=[pl.BlockSpec((SEQ, gw), lambda b, g: (b, mcol + g)),
                   pl.BlockSpec((SEQ, LANES), lambda b, g: (b, g))],
        out_shape=[jax.ShapeDtypeStruct((N_TOK, MIX_WIDTH), BF16),
                   jax.ShapeDtypeStruct((N_CTX, GQA_KV_HEADS * HEAD_DIM), F32)],
        input_output_aliases={0: 0},
        compiler_params=_params("arbitrary", "arbitrary"),
        name="gqa_ctx",
    )(mix, qg, kg, proj, proj, proj)

    nq = DEC_SEQ // ATT_TQ
    q_row = lambda b, qi: (N_CTX // ATT_TQ) + b * nq + qi
    kv_row = lambda b: (N_CTX // DEC_SEQ) + b
    ck = cache_k.reshape(DEC_BATCH, DEPTH, PAST_LEN, GQA_KV_HEADS * HEAD_DIM)
    cv = cache_v.reshape(DEC_BATCH, DEPTH, PAST_LEN, GQA_KV_HEADS * HEAD_DIM)
    tab_q = pl.BlockSpec((ATT_TQ, LANES), lambda b, g, qi: (qi, 0))
    tab_k = pl.BlockSpec((DEC_SEQ, LANES), lambda b, g, qi: (0, 0))
    mix = pl.pallas_call(
        functools.partial(_gqa_kernel, latent=True),
        grid=(DEC_BATCH, GQA_KV_HEADS, nq),
        in_specs=[_ANY] + small + [
            pl.BlockSpec((ATT_TQ, gw), lambda b, g, qi: (q_row(b, qi), qcol + g)),
            pl.BlockSpec((DEC_SEQ, LANES), lambda b, g, qi: (kv_row(b), COL_KB + g)),
            pl.BlockSpec((DEC_SEQ, LANES), lambda b, g, qi: (kv_row(b), COL_VB + g)),
            pl.BlockSpec((None, None, PAST_LEN, LANES), lambda b, g, qi: (b, layer, 0, g)),
            pl.BlockSpec((None, None, PAST_LEN, LANES), lambda b, g, qi: (b, layer, 0, g)),
            tab_q, tab_q, tab_q, tab_k, tab_k, tab_k,
        ],
        out_specs=pl.BlockSpec((ATT_TQ, gw), lambda b, g, qi: (q_row(b, qi), mcol + g)),
        out_shape=jax.ShapeDtypeStruct((N_TOK, MIX_WIDTH), BF16),
        input_output_aliases={0: 0},
        compiler_params=_params("arbitrary", "arbitrary", "arbitrary"),
        name="gqa_lat",
    )(mix, qg, kg, proj, proj, proj, ck, cv, *tabs128, *tabs128)
    return mix, kn


def _ret_kernel(*refs, seq, latent):
    refs = refs[1:]
    if latent:
        (lgf_ref, lgb_ref, q_ref, k_ref, v_ref, gc_ref, s0f_ref, s0b_ref,
         cq_ref, saq_ref, sbq_ref, ck_ref, sak_ref, sbk_ref, o_ref) = refs
    else:
        lgf_ref, lgb_ref, q_ref, k_ref, v_ref, gc_ref, o_ref, sf_ref, sb_ref = refs
    hp = pl.program_id(1)
    qi = pl.program_id(2)
    q = q_ref[...]
    k = k_ref[...] * (RET_QK_DIM ** -0.5)
    if latent:
        half = RET_QK_DIM // 2
        q = _apply_rope(q, cq_ref[...], saq_ref[...], sbq_ref[...], half)
        k = _apply_rope(k, ck_ref[...], sak_ref[...], sbk_ref[...], half)
    tq = q.shape[0]
    t_idx = qi * tq + lax.broadcasted_iota(jnp.int32, (tq, seq), 0)
    s_idx = lax.broadcasted_iota(jnp.int32, (tq, seq), 1)
    rel = (t_idx - s_idx).astype(F32)
    t_col = (qi * tq + lax.broadcasted_iota(jnp.int32, (tq, 1), 0)).astype(F32)
    s_col = lax.broadcasted_iota(jnp.int32, (seq, 1), 0).astype(F32)
    lane_q = lax.broadcasted_iota(jnp.int32, q.shape, 1)
    lane_k = lax.broadcasted_iota(jnp.int32, k.shape, 1)
    kb = k.astype(BF16)
    for j in range(2):
        lgf = lgf_ref[2 * hp + j]
        lgb = lgb_ref[2 * hp + j]
        decay = (jnp.where(rel >= 0, jnp.exp(jnp.maximum(rel, 0.0) * lgf), 0.0)
                 + jnp.where(rel <= 0, jnp.exp(jnp.maximum(-rel, 0.0) * lgb), 0.0))
        in_head_q = (lane_q >= j * RET_QK_DIM) & (lane_q < (j + 1) * RET_QK_DIM)
        qj = jnp.where(in_head_q, q, 0.0).astype(BF16)
        vj = v_ref[:, j * HEAD_DIM:(j + 1) * HEAD_DIM].astype(BF16)
        scores = lax.dot_general(qj, kb, _NT, preferred_element_type=F32) * decay
        o = jnp.dot(scores.astype(BF16), vj, preferred_element_type=F32)
        if latent:
            o = o + (jnp.dot(qj, s0f_ref[...].astype(BF16), preferred_element_type=F32)
                     * jnp.exp((t_col + 1.0) * lgf))
            o = o + (jnp.dot(qj, s0b_ref[...].astype(BF16), preferred_element_type=F32)
                     * jnp.exp((seq - t_col) * lgb))
        else:
            in_head_k = (lane_k >= j * RET_QK_DIM) & (lane_k < (j + 1) * RET_QK_DIM)
            kj = jnp.where(in_head_k, k, 0.0)
            kf = (kj * jnp.exp((seq - 1.0 - s_col) * lgf)).astype(BF16)
            kr = (kj * jnp.exp(s_col * lgb)).astype(BF16)
            sf = lax.dot_general(kf, vj, _TN, preferred_element_type=F32)
            sb = lax.dot_general(kr, vj, _TN, preferred_element_type=F32)
            sf_ref[0, j] = sf[j * RET_QK_DIM:(j + 1) * RET_QK_DIM]
            sb_ref[0, j] = sb[j * RET_QK_DIM:(j + 1) * RET_QK_DIM]
        mu = jnp.mean(o, axis=-1, keepdims=True)
        oc = o - mu
        var = jnp.mean(oc * oc, axis=-1, keepdims=True)
        gcj = gc_ref[:, j * HEAD_DIM:(j + 1) * HEAD_DIM]
        silu = gcj * (1.0 / (1.0 + jnp.exp(-gcj)))
        o_ref[:, j * HEAD_DIM:(j + 1) * HEAD_DIM] = (oc * lax.rsqrt(var + NORM_EPS) * silu).astype(o_ref.dtype)


def _retention(mix, proj, lg_f, lg_b, layer, state_f, state_b, tabs64):
    pairs = RET_HEADS // 2
    pw = 2 * HEAD_DIM
    smem = pl.BlockSpec(memory_space=pltpu.SMEM)
    vcol = COL_VC * LANES // pw
    gcol = COL_GC * LANES // pw
    mcol = (DIFF_HEADS + GQA_Q_HEADS) * HEAD_DIM // pw
    mix, sf, sb = pl.pallas_call(
        functools.partial(_ret_kernel, seq=SEQ, latent=False),
        grid=(BATCH, pairs, 1),
        in_specs=[
            _ANY, smem, smem,
            pl.BlockSpec((SEQ, LANES), lambda b, p, qi: (b, COL_QC + p)),
            pl.BlockSpec((SEQ, LANES), lambda b, p, qi: (b, COL_KC + p)),
            pl.BlockSpec((SEQ, pw), lambda b, p, qi: (b, vcol + p)),
            pl.BlockSpec((SEQ, pw), lambda b, p, qi: (b, gcol + p)),
        ],
        out_specs=[
            pl.BlockSpec((SEQ, pw), lambda b, p, qi: (b, mcol + p)),
            pl.BlockSpec((1, 2, RET_QK_DIM, HEAD_DIM), lambda b, p, qi: (b, p, 0, 0)),
            pl.BlockSpec((1, 2, RET_QK_DIM, HEAD_DIM), lambda b, p, qi: (b, p, 0, 0)),
        ],
        out_shape=[
            jax.ShapeDtypeStruct((N_TOK, MIX_WIDTH), BF16),
            jax.ShapeDtypeStruct((BATCH, RET_HEADS, RET_QK_DIM, HEAD_DIM), F32),
            jax.ShapeDtypeStruct((BATCH, RET_HEADS, RET_QK_DIM, HEAD_DIM), F32),
        ],
        input_output_aliases={0: 0},
        compiler_params=_params("arbitrary", "arbitrary", "arbitrary"),
        name="ret_ctx",
    )(mix, lg_f, lg_b, proj, proj, proj, proj)

    nq = DEC_SEQ // ATT_TQ
    q_row = lambda b, qi: (N_CTX // ATT_TQ) + b * nq + qi
    kv_row = lambda b: (N_CTX // DEC_SEQ) + b
    s0f = state_f.reshape(DEC_BATCH, DEPTH, pairs, 2 * RET_QK_DIM, HEAD_DIM)
    s0b = state_b.reshape(DEC_BATCH, DEPTH, pairs, 2 * RET_QK_DIM, HEAD_DIM)
    tab_q = pl.BlockSpec((ATT_TQ, LANES), lambda b, p, qi: (qi, 0))
    tab_k = pl.BlockSpec((DEC_SEQ, LANES), lambda b, p, qi: (0, 0))
    state_spec = pl.BlockSpec((None, None, None, 2 * RET_QK_DIM, HEAD_DIM), lambda b, p, qi: (b, layer, p, 0, 0))
    mix = pl.pallas_call(
        functools.partial(_ret_kernel, seq=DEC_SEQ, latent=True),
        grid=(DEC_BATCH, pairs, nq),
        in_specs=[
            _ANY, smem, smem,
            pl.BlockSpec((ATT_TQ, LANES), lambda b, p, qi: (q_row(b, qi), COL_QC + p)),
            pl.BlockSpec((DEC_SEQ, LANES), lambda b, p, qi: (kv_row(b), COL_KC + p)),
            pl.BlockSpec((DEC_SEQ, pw), lambda b, p, qi: (kv_row(b), vcol + p)),
            pl.BlockSpec((ATT_TQ, pw), lambda b, p, qi: (q_row(b, qi), gcol + p)),
            state_spec, state_spec,
            tab_q, tab_q, tab_q, tab_k, tab_k, tab_k,
        ],
        out_specs=pl.BlockSpec((ATT_TQ, pw), lambda b, p, qi: (q_row(b, qi), mcol + p)),
        out_shape=jax.ShapeDtypeStruct((N_TOK, MIX_WIDTH), BF16),
        input_output_aliases={0: 0},
        compiler_params=_params("arbitrary", "arbitrary", "arbitrary"),
        name="ret_lat",
    )(mix, lg_f, lg_b, proj, proj, proj, proj, s0f, s0b, *tabs64, *tabs64)
    return mix, sf, sb


def _router_kernel(x_ref, g_ref, sh_ref, sc_ref, rw_ref, rb_ref, h_ref, e_ref, gate_ref):
    h = _ada_norm_value(x_ref[...], g_ref[...], sh_ref[0], sc_ref[0])
    h_ref[...] = h
    hh = h.astype(BF16)
    hl = (h - hh.astype(F32)).astype(BF16)
    w = rw_ref[...]
    wh = w.astype(BF16)
    wl = (w - wh.astype(F32)).astype(BF16)
    logits = (jnp.dot(hh, wh, preferred_element_type=F32) + jnp.dot(hh, wl, preferred_element_type=F32)
              + jnp.dot(hl, wh, preferred_element_type=F32) + rb_ref[...])
    lane = lax.broadcasted_iota(jnp.int32, logits.shape, 1).astype(F32)
    wide = lax.broadcasted_iota(jnp.int32, e_ref.shape, 1)
    e_out = jnp.zeros(e_ref.shape, F32)
    v_out = jnp.zeros(e_ref.shape, F32)
    top0 = None
    den = None
    for kk in range(TOP_K):
        m = jnp.max(logits, axis=-1, keepdims=True)
        idx = jnp.min(jnp.where(logits == m, lane, float(N_EXPERTS)), axis=-1, keepdims=True)
        logits = jnp.where(lane == idx, -jnp.inf, logits)
        if kk == 0:
            top0 = m
        ex = jnp.exp(m - top0)
        den = ex if den is None else den + ex
        e_out = jnp.where(wide == kk, idx, e_out)
        v_out = jnp.where(wide == kk, ex, v_out)
    e_ref[...] = e_out.astype(jnp.int32)
    gate_ref[...] = v_out / den


def _router(x, g, modt, layer, router_w, router_b):
    tm = 256
    sh0 = _mod_index(layer, 3)
    sc0 = _mod_index(layer, 4)
    return pl.pallas_call(
        _router_kernel,
        grid=(N_TOK // tm,),
        in_specs=[
            pl.BlockSpec((tm, D_MODEL), lambda i: (i, 0)),
            pl.BlockSpec((1, D_MODEL), lambda i: (0, 0)),
            pl.BlockSpec((1, 1, D_MODEL), lambda i: (sh0 + _stream_of_row_tile(i, tm), 0, 0)),
            pl.BlockSpec((1, 1, D_MODEL), lambda i: (sc0 + _stream_of_row_tile(i, tm), 0, 0)),
            pl.BlockSpec((None, D_MODEL, N_EXPERTS), lambda i: (layer, 0, 0)),
            pl.BlockSpec((None, 1, N_EXPERTS), lambda i: (layer, 0, 0)),
        ],
        out_specs=[
            pl.BlockSpec((tm, D_MODEL), lambda i: (i, 0)),
            pl.BlockSpec((tm, LANES), lambda i: (i, 0)),
            pl.BlockSpec((tm, LANES), lambda i: (i, 0)),
        ],
        out_shape=[
            jax.ShapeDtypeStruct((N_TOK, D_MODEL), F32),
            jax.ShapeDtypeStruct((N_TOK, LANES), jnp.int32),
            jax.ShapeDtypeStruct((N_TOK, LANES), F32),
        ],
        compiler_params=_params("arbitrary"),
        name="router",
    )(x, g.reshape(1, D_MODEL), modt, modt, router_w, router_b.reshape(DEPTH, 1, N_EXPERTS))


def _route_meta(top_e):
    n_assign = N_TOK * TOP_K
    n_items = -(-n_assign // MOE_ROWS) + N_EXPERTS
    flat_e = top_e.reshape(n_assign)
    order = jnp.argsort(flat_e).astype(jnp.int32)
    inv = jnp.argsort(order).astype(jnp.int32)
    onehot = (flat_e[:, None] == jnp.arange(N_EXPERTS, dtype=jnp.int32)[None, :]).astype(jnp.int32)
    counts = jnp.sum(onehot, axis=0)
    padded = (counts + MOE_ROWS - 1) // MOE_ROWS * MOE_ROWS
    start = jnp.cumsum(counts) - counts
    pad_end = jnp.cumsum(padded)
    pad_start = pad_end - padded
    slot_of = inv + jnp.sum(onehot * (pad_start - start)[None, :], axis=1)
    tok_sorted = jnp.concatenate([order // TOP_K, jnp.zeros((MOE_ROWS,), jnp.int32)])
    item_start = jnp.arange(n_items, dtype=jnp.int32) * MOE_ROWS
    item_e = jnp.minimum(jnp.searchsorted(pad_end, item_start, side='right'), N_EXPERTS - 1).astype(jnp.int32)
    off = item_start - pad_start[item_e]
    item_rows = jnp.clip(counts[item_e] - off, 0, MOE_ROWS).astype(jnp.int32)
    item_src = jnp.clip(start[item_e] + off, 0, n_assign - 1).astype(jnp.int32)
    n_act = (pad_end[-1] // MOE_ROWS).astype(jnp.int32).reshape(1)
    return tok_sorted, slot_of.astype(jnp.int32), (item_e, item_rows, item_src, n_act), n_items


def _expert_kernel(ie_ref, rows_ref, src_ref, nact_ref, tok_ref,
                   h_hbm, wg_ref, wu_ref, bg_ref, bu_ref, wd_ref, bd_ref, o_ref, x_ref, act_ref, sem):
    it = pl.program_id(0)
    s = pl.program_id(1)
    n_act = nact_ref[0]
    active = it < n_act
    rows = rows_ref[it]
    n_full = rows // MOE_SUB
    rem = rows - n_full * MOE_SUB
    n_main = n_full + (rem > MOE_SUB // 2).astype(jnp.int32)
    tail0 = pl.multiple_of(n_full * MOE_SUB, MOE_SUB)
    short_tail = (rem > 0) & (rem <= MOE_SUB // 4)
    half_tail = (rem > MOE_SUB // 4) & (rem <= MOE_SUB // 2)

    def row_copy(tok, r):
        return pltpu.make_async_copy(h_hbm.at[pl.ds(tok, 1), :], x_ref.at[pl.ds(r, 1), :], sem.at[0])

    def issue_static(item, r0, n):
        base = src_ref[item]
        for u in range(n):
            row_copy(tok_ref[base + r0 + u], r0 + u).start()

    def issue(item, lo, hi):
        def body(g, carry):
            issue_static(item, pl.multiple_of(g * 8, 8), 8)
            return carry
        lax.fori_loop(lo // 8, hi // 8, body, 0)

    def wait_all_rows():
        def wait_body(g, carry):
            for u in range(8):
                row_copy(0, 0).wait()
            return carry
        lax.fori_loop(0, MOE_ROWS // 8, wait_body, 0)

    @pl.when(active & (s == 0))
    def _():
        @pl.when(it == 0)
        def _():
            issue(0, 0, MOE_ROWS)
        wait_all_rows()

    def gate_up(r0, m):
        xs = x_ref[pl.ds(r0, m), :].astype(BF16)
        gate = jnp.dot(xs, wg_ref[...].astype(BF16), preferred_element_type=F32) + bg_ref[...]
        up = jnp.dot(xs, wu_ref[...].astype(BF16), preferred_element_type=F32) + bu_ref[...]
        gate = jnp.minimum(gate, SWIGLU_LIMIT)
        up = jnp.clip(up, -SWIGLU_LIMIT, SWIGLU_LIMIT)
        a = (up + 1.0) * gate * (1.0 / (1.0 + jnp.exp(-SWIGLU_ALPHA * gate)))
        act_ref[s, pl.ds(r0, m), :] = a.astype(BF16)

    def down(r0, m):
        acc = None
        for f in range(MOE_NF):
            d = jnp.dot(act_ref[f, pl.ds(r0, m), :], wd_ref[f * MOE_TF:(f + 1) * MOE_TF, :].astype(BF16),
                        preferred_element_type=F32)
            acc = d if acc is None else acc + d
        o_ref[pl.ds(r0, m), :] = acc + bd_ref[...]

    def run_rows(fn, before_block=None):
        def body(i, carry):
            if before_block is not None:
                before_block(i)
            fn(pl.multiple_of(i * 2 * MOE_SUB, 2 * MOE_SUB), 2 * MOE_SUB)
            return carry
        lax.fori_loop(0, n_main // 2, body, 0)

        @pl.when(n_main % 2 == 1)
        def _():
            if before_block is not None:
                before_block(n_main // 2)
            fn(pl.multiple_of((n_main - 1) * MOE_SUB, MOE_SUB), MOE_SUB)

        @pl.when(short_tail)
        def _():
            fn(tail0, MOE_SUB // 4)

        @pl.when(half_tail)
        def _():
            fn(tail0, MOE_SUB // 2)

    @pl.when(active & (s < MOE_NF))
    def _():
        run_rows(gate_up)

    @pl.when(active & (s >= MOE_NF))
    def _():
        nxt = jnp.minimum(it + 1, n_act - 1)
        row_d = (s - MOE_NF) * MOE_ISSUE
        share = MOE_ISSUE // 2

        def issue_share(k):
            issue_static(nxt, pl.multiple_of(row_d + k * share, share), share)
        run_rows(down, issue_share)
        started = share * jnp.minimum(n_main // 2 + n_main % 2, 2)
        issue(nxt, row_d + started, row_d + MOE_ISSUE)

        @pl.when((it == n_act - 1) & (s == MOE_NF + MOE_ND - 1))
        def _():
            wait_all_rows()


def _experts(h, tok_sorted, item_meta, n_items, layer, w_gu, b_gu, w_dn, b_dn):
    n_steps = MOE_NF + MOE_ND
    cap = n_items * MOE_ROWS

    def pos(it, s, ie, nact):
        last = jnp.maximum(nact[0] - 1, 0)
        itc = jnp.minimum(it, last)
        sc = jnp.where(it < nact[0], s, n_steps - 1)
        return itc, sc, ie[itc]

    def wg_map(it, s, ie, rows, src, nact, tok):
        _, sc, e = pos(it, s, ie, nact)
        return (layer, e, 0, jnp.minimum(sc, MOE_NF - 1))

    def wu_map(it, s, ie, rows, src, nact, tok):
        _, sc, e = pos(it, s, ie, nact)
        return (layer, e, 0, MOE_NF + jnp.minimum(sc, MOE_NF - 1))

    def wd_map(it, s, ie, rows, src, nact, tok):
        _, sc, e = pos(it, s, ie, nact)
        return (layer, e, 0, jnp.clip(sc - MOE_NF, 0, MOE_ND - 1))

    def o_map(it, s, ie, rows, src, nact, tok):
        itc, sc, _ = pos(it, s, ie, nact)
        return (itc, jnp.clip(sc - MOE_NF, 0, MOE_ND - 1))

    grid_spec = pltpu.PrefetchScalarGridSpec(
        num_scalar_prefetch=5,
        grid=(n_items, n_steps),
        in_specs=[
            _ANY,
            pl.BlockSpec((None, None, D_MODEL, MOE_TF), wg_map),
            pl.BlockSpec((None, None, D_MODEL, MOE_TF), wu_map),
            pl.BlockSpec((None, None, 1, MOE_TF), wg_map),
            pl.BlockSpec((None, None, 1, MOE_TF), wu_map),
            pl.BlockSpec((None, None, EXPERT_FF, MOE_TD), wd_map),
            pl.BlockSpec((None, None, 1, MOE_TD), wd_map),
        ],
        out_specs=pl.BlockSpec((MOE_ROWS, MOE_TD), o_map),
        scratch_shapes=[
            pltpu.VMEM((MOE_ROWS, D_MODEL), F32),
            pltpu.VMEM((MOE_NF, MOE_ROWS, MOE_TF), BF16),
            pltpu.SemaphoreType.DMA((1,)),
        ],
    )
    b_gu4 = b_gu.reshape(DEPTH, N_EXPERTS, 1, 2 * EXPERT_FF)
    return pl.pallas_call(
        _expert_kernel,
        grid_spec=grid_spec,
        out_shape=jax.ShapeDtypeStruct((cap, D_MODEL), F32),
        compiler_params=_params("arbitrary", "arbitrary"),
        name="moe_experts",
    )(*item_meta, tok_sorted, h, w_gu, w_gu, b_gu4, b_gu4, w_dn, b_dn.reshape(DEPTH, N_EXPERTS, 1, D_MODEL))


def _combine_kernel(slot_ref, y_hbm, x_ref, gates_ref, g2_ref, fg_ref, o_ref, buf, sem, *, final_norm):
    i = pl.program_id(0)
    n = pl.num_programs(0)

    def copy(tile, slot, r, kk):
        row = slot_ref[(tile * COMBINE_TOK + r) * TOP_K + kk]
        return pltpu.make_async_copy(y_hbm.at[pl.ds(row, 1), :], buf.at[slot, kk, pl.ds(r, 1), :], sem.at[slot])

    def issue(tile, slot):
        def body(r, carry):
            for kk in range(TOP_K):
                copy(tile, slot, r, kk).start()
            return carry
        lax.fori_loop(0, COMBINE_TOK, body, 0)

    @pl.when(i == 0)
    def _():
        issue(0, 0)

    @pl.when(i + 1 < n)
    def _():
        issue(i + 1, (i + 1) % 2)

    slot = i % 2

    def wait_body(r, carry):
        for kk in range(TOP_K):
            copy(i, slot, r, kk).wait()
        return carry
    lax.fori_loop(0, COMBINE_TOK, wait_body, 0)

    gates = gates_ref[...]
    y = gates[:, 0:1] * buf[slot, 0]
    for kk in range(1, TOP_K):
        y = y + gates[:, kk:kk + 1] * buf[slot, kk]
    out = x_ref[...] + g2_ref[0] * y
    if final_norm:
        ms = jnp.mean(out * out, axis=-1, keepdims=True)
        out = out * lax.rsqrt(ms + NORM_EPS) * fg_ref[...]
    o_ref[...] = out


def _combine(yb, slot_of, x, gates, modt, layer, final_g, final_norm):
    tc = COMBINE_TOK
    g0 = _mod_index(layer, 5)
    grid_spec = pltpu.PrefetchScalarGridSpec(
        num_scalar_prefetch=1,
        grid=(N_TOK // tc,),
        in_specs=[
            pl.BlockSpec(memory_space=pl.ANY),
            pl.BlockSpec((tc, D_MODEL), lambda i, sl: (i, 0)),
            pl.BlockSpec((tc, LANES), lambda i, sl: (i, 0)),
            pl.BlockSpec((1, 1, D_MODEL), lambda i, sl: (g0 + _stream_of_row_tile(i, tc), 0, 0)),
            pl.BlockSpec((1, D_MODEL), lambda i, sl: (0, 0)),
        ],
        out_specs=pl.BlockSpec((tc, D_MODEL), lambda i, sl: (i, 0)),
        scratch_shapes=[pltpu.VMEM((2, TOP_K, tc, D_MODEL), F32), pltpu.SemaphoreType.DMA((2,))],
    )
    return pl.pallas_call(
        functools.partial(_combine_kernel, final_norm=final_norm),
        grid_spec=grid_spec,
        out_shape=jax.ShapeDtypeStruct((N_TOK, D_MODEL), F32),
        compiler_params=_params("arbitrary"),
        name="moe_combine",
    )(slot_of, yb, x, gates, modt, final_g.reshape(1, D_MODEL))


def kernel(x_prompt, x_sample, cache_diff_k, cache_diff_v, cache_gqa_k, cache_gqa_v, state_ret_fwd, state_ret_bwd, c, c_ctx, w_mod, b_mod, norm_mix_g, norm_ffn_g, w_in, w_out, diff_lambda, diff_subln_g, gqa_q_norm_g, gqa_k_norm_g, ret_log_decay_fwd, ret_log_decay_bwd, router_w, router_b, moe_w_gate_up, moe_b_gate_up, moe_w_down, moe_b_down, final_norm_g):
    x = jnp.concatenate([x_prompt.reshape(N_CTX, D_MODEL), x_sample.reshape(N_LAT, D_MODEL)], axis=0)
    cond_raw = jnp.concatenate([c_ctx[None].astype(F32), c.astype(F32),
                                jnp.zeros((8 - N_STREAM, D_MODEL), F32)], axis=0)
    modt = _modulation(cond_raw, w_mod, b_mod)
    tabs64 = _rope_tables(DEC_SEQ, DIFF_QK_DIM)
    tabs128 = _rope_tables(DEC_SEQ, HEAD_DIM)

    new_dk, new_dv, new_gk, new_gv, new_sf, new_sb = [], [], [], [], [], []
    for layer in range(DEPTH):
        lam_init = 0.8 - 0.6 * math.exp(-0.3 * layer)
        h = _adanorm(x, norm_mix_g[layer], modt, layer, 0, 1)
        proj = _in_proj(h, w_in[layer])
        mix = _diff_attention(proj, diff_lambda[layer], diff_subln_g[layer], lam_init, layer,
                              cache_diff_k, cache_diff_v, tabs64)
        mix, kn = _gqa_attention(mix, proj, gqa_q_norm_g[layer], gqa_k_norm_g[layer], layer,
                                 cache_gqa_k, cache_gqa_v, tabs128)
        mix, sf, sb = _retention(mix, proj, ret_log_decay_fwd[layer], ret_log_decay_bwd[layer], layer,
                                 state_ret_fwd, state_ret_bwd, tabs64)
        x = _out_proj(mix, w_out[layer], x, modt, layer, 2)

        h2, top_e, gates = _router(x, norm_ffn_g[layer], modt, layer, router_w, router_b)
        tok_sorted, slot_of, item_meta, n_items = _route_meta(top_e[:, :TOP_K])
        yb = _experts(h2, tok_sorted, item_meta, n_items, layer,
                      moe_w_gate_up, moe_b_gate_up, moe_w_down, moe_b_down)
        x = _combine(yb, slot_of, x, gates, modt, layer, final_norm_g, layer == DEPTH - 1)

        pc = proj[:N_CTX]
        new_dk.append(pc[:, COL_KA * LANES:COL_VA * LANES].reshape(BATCH, SEQ, DIFF_HEADS, 2, DIFF_QK_DIM))
        new_dv.append(pc[:, COL_VA * LANES:COL_QB * LANES].reshape(BATCH, SEQ, DIFF_HEADS, HEAD_DIM))
        new_gk.append(kn.reshape(BATCH, SEQ, GQA_KV_HEADS, HEAD_DIM))
        new_gv.append(pc[:, COL_VB * LANES:COL_QC * LANES].reshape(BATCH, SEQ, GQA_KV_HEADS, HEAD_DIM))
        new_sf.append(sf)
        new_sb.append(sb)

    y_prompt = x[:N_CTX].reshape(BATCH, SEQ, D_MODEL)
    y_sample = x[N_CTX:].reshape(DEC_BATCH, DEC_SEQ, D_MODEL)
    return (y_prompt, y_sample, jnp.stack(new_dk, axis=1), jnp.stack(new_dv, axis=1),
            jnp.stack(new_gk, axis=1), jnp.stack(new_gv, axis=1),
            jnp.stack(new_sf, axis=1), jnp.stack(new_sb, axis=1))
```

```python
import functools
import math

import jax
import jax.numpy as jnp
from jax import lax
from jax.experimental import pallas as pl
from jax.experimental.pallas import tpu as pltpu

F32 = jnp.float32
BF16 = jnp.bfloat16

D_MODEL = 4096
BATCH = 16
SEQ = 256
DEPTH = 2
DEC_BATCH = 2
DEC_SEQ = 1024
PAST_LEN = 512
GRID_W = 64
HEAD_DIM = 128
DIFF_HEADS = 8
DIFF_QK_DIM = 64
GQA_Q_HEADS = 16
GQA_KV_HEADS = 4
GQA_GROUP = 4
RET_HEADS = 8
RET_QK_DIM = 64
N_EXPERTS = 32
TOP_K = 4
EXPERT_FF = D_MODEL // 2
SWIGLU_LIMIT = 7.0
SWIGLU_ALPHA = 1.702
ROPE_THETA = 10000.0
NORM_EPS = 1e-6
N_MOD = 6

N_CTX = BATCH * SEQ
N_LAT = DEC_BATCH * DEC_SEQ
N_TOK = N_CTX + N_LAT
N_STREAM = 1 + DEC_BATCH
IN_WIDTH = 9216
MIX_WIDTH = 4096

LANES = 128
COL_QA, COL_KA, COL_VA, COL_QB, COL_KB, COL_VB, COL_QC, COL_KC, COL_VC, COL_GC = 0, 8, 16, 24, 40, 44, 48, 52, 56, 64

ATT_TQ = 256
DIFF_HB = 2
DIFF_HB_CTX = 4
GQA_GB_CTX = 2
MOE_ROWS = 1024
MOE_SUB = 256
MOE_TF = 256
MOE_TD = 512
MOE_NF = EXPERT_FF // MOE_TF
MOE_ND = D_MODEL // MOE_TD
MOE_ISSUE = MOE_ROWS // MOE_ND
COMBINE_TOK = 128
VMEM_LIMIT = 60 * 1024 * 1024


def _params(*sem):
    return pltpu.CompilerParams(dimension_semantics=sem, vmem_limit_bytes=VMEM_LIMIT)


def _stream_of_row_tile(i, tm):
    r = i * tm
    return jnp.where(r < N_CTX, 0, 1 + (r - N_CTX) // DEC_SEQ)


def _mod_index(layer, which):
    return (layer * N_MOD + which) * N_STREAM


def _mod_kernel(c_ref, w_ref, b_ref, o_ref):
    c = c_ref[...]
    cond = c * (1.0 / (1.0 + jnp.exp(-c)))
    o_ref[0] = jnp.dot(cond.astype(BF16), w_ref[0].astype(BF16), preferred_element_type=F32) + b_ref[0]


def _modulation(cond_raw, w_mod, b_mod):
    tn = 1024
    n_out = N_MOD * D_MODEL
    out = pl.pallas_call(
        _mod_kernel,
        grid=(DEPTH, n_out // tn),
        in_specs=[
            pl.BlockSpec((8, D_MODEL), lambda l, j: (0, 0)),
            pl.BlockSpec((1, D_MODEL, tn), lambda l, j: (l, 0, j)),
            pl.BlockSpec((1, 1, tn), lambda l, j: (l, 0, j)),
        ],
        out_specs=pl.BlockSpec((1, 8, tn), lambda l, j: (l, 0, j)),
        out_shape=jax.ShapeDtypeStruct((DEPTH, 8, n_out), F32),
        compiler_params=_params("arbitrary", "arbitrary"),
        name="modulation",
    )(cond_raw, w_mod, b_mod.reshape(DEPTH, 1, n_out))
    out = out.reshape(DEPTH, 8, N_MOD, D_MODEL)[:, :N_STREAM]
    return out.transpose(0, 2, 1, 3).reshape(DEPTH * N_MOD * N_STREAM, 1, D_MODEL)


def _ada_norm_value(x, g, shift, scale):
    ms = jnp.mean(x * x, axis=-1, keepdims=True)
    y = x * lax.rsqrt(ms + NORM_EPS) * g
    return y * (1.0 + scale) + shift


def _adanorm_kernel(x_ref, g_ref, sh_ref, sc_ref, o_ref):
    o_ref[...] = _ada_norm_value(x_ref[...], g_ref[...], sh_ref[0], sc_ref[0]).astype(o_ref.dtype)


def _adanorm(x, g, modt, layer, which_shift, which_scale):
    tm = 256
    sh0 = _mod_index(layer, which_shift)
    sc0 = _mod_index(layer, which_scale)
    return pl.pallas_call(
        _adanorm_kernel,
        grid=(N_TOK // tm,),
        in_specs=[
            pl.BlockSpec((tm, D_MODEL), lambda i: (i, 0)),
            pl.BlockSpec((1, D_MODEL), lambda i: (0, 0)),
            pl.BlockSpec((1, 1, D_MODEL), lambda i: (sh0 + _stream_of_row_tile(i, tm), 0, 0)),
            pl.BlockSpec((1, 1, D_MODEL), lambda i: (sc0 + _stream_of_row_tile(i, tm), 0, 0)),
        ],
        out_specs=pl.BlockSpec((tm, D_MODEL), lambda i: (i, 0)),
        out_shape=jax.ShapeDtypeStruct((N_TOK, D_MODEL), BF16),
        compiler_params=_params("arbitrary"),
        name="adanorm",
    )(x, g.reshape(1, D_MODEL), modt, modt)


def _mm_kernel(x_ref, w_ref, o_ref, wb_ref):
    @pl.when(pl.program_id(1) == 0)
    def _():
        wb_ref[...] = w_ref[...].astype(BF16)

    o_ref[...] = jnp.dot(x_ref[...], wb_ref[...], preferred_element_type=F32)


def _mm_residual_kernel(x_ref, w_ref, res_ref, gate_ref, o_ref, wb_ref):
    @pl.when(pl.program_id(1) == 0)
    def _():
        wb_ref[...] = w_ref[...].astype(BF16)

    acc = jnp.dot(x_ref[...], wb_ref[...], preferred_element_type=F32)
    o_ref[...] = res_ref[...] + gate_ref[0] * acc


def _in_proj(h, w):
    tm, tn = 1024, 512
    k = h.shape[1]
    n = w.shape[1]
    return pl.pallas_call(
        _mm_kernel,
        grid=(n // tn, N_TOK // tm),
        in_specs=[
            pl.BlockSpec((tm, k), lambda j, i: (i, 0)),
            pl.BlockSpec((k, tn), lambda j, i: (0, j)),
        ],
        out_specs=pl.BlockSpec((tm, tn), lambda j, i: (i, j)),
        out_shape=jax.ShapeDtypeStruct((N_TOK, n), F32),
        scratch_shapes=[pltpu.VMEM((k, tn), BF16)],
        compiler_params=_params("arbitrary", "arbitrary"),
        name="in_proj",
    )(h, w)


def _out_proj(mix, w, res, modt, layer, which_gate):
    tm, tn = 1024, 512
    k = mix.shape[1]
    n = w.shape[1]
    g0 = _mod_index(layer, which_gate)
    return pl.pallas_call(
        _mm_residual_kernel,
        grid=(n // tn, N_TOK // tm),
        in_specs=[
            pl.BlockSpec((tm, k), lambda j, i: (i, 0)),
            pl.BlockSpec((k, tn), lambda j, i: (0, j)),
            pl.BlockSpec((tm, tn), lambda j, i: (i, j)),
            pl.BlockSpec((1, 1, tn), lambda j, i: (g0 + _stream_of_row_tile(i, tm), 0, j)),
        ],
        out_specs=pl.BlockSpec((tm, tn), lambda j, i: (i, j)),
        out_shape=jax.ShapeDtypeStruct((N_TOK, n), F32),
        scratch_shapes=[pltpu.VMEM((k, tn), BF16)],
        compiler_params=_params("arbitrary", "arbitrary"),
        name="out_proj",
    )(mix, w, res, modt)


def _rope_tables(n_tokens, unit):
    rows = n_tokens // GRID_W
    row = jnp.repeat(jnp.arange(rows), GRID_W).astype(F32)
    col = jnp.tile(jnp.arange(GRID_W), rows).astype(F32)
    n_freq = unit // 4
    inv_freq = jnp.power(ROPE_THETA, -jnp.arange(n_freq, dtype=F32) / n_freq)
    ang = jnp.concatenate([row[:, None] * inv_freq, col[:, None] * inv_freq], axis=-1)
    cos, sin = jnp.cos(ang), jnp.sin(ang)
    zero = jnp.zeros_like(sin)
    reps = LANES // unit
    c = jnp.tile(jnp.concatenate([cos, cos], axis=-1), (1, reps))
    sa = jnp.tile(jnp.concatenate([-sin, zero], axis=-1), (1, reps))
    sb = jnp.tile(jnp.concatenate([zero, sin], axis=-1), (1, reps))
    return c, sa, sb


def _apply_rope(x, c, sa, sb, half):
    return x * c + pltpu.roll(x, LANES - half, 1) * sa + pltpu.roll(x, half, 1) * sb


_ANY = pl.BlockSpec(memory_space=pl.ANY)
_NT = (((1,), (1,)), ((), ()))
_TN = (((0,), (0,)), ((), ()))


def _softmax_pv(score_list, value_list):
    m = score_list[0].max(axis=-1, keepdims=True)
    for s in score_list[1:]:
        m = jnp.maximum(m, s.max(axis=-1, keepdims=True))
    acc = None
    den = None
    for s, v in zip(score_list, value_list):
        p = jnp.exp(s - m)
        d = p.sum(axis=-1, keepdims=True)
        o = jnp.dot(p.astype(BF16), v, preferred_element_type=F32)
        acc = o if acc is None else acc + o
        den = d if den is None else den + d
    return acc / den


def _diff_kernel(*refs, lam_init, latent, hb):
    if latent:
        refs = refs[1:]
        (lamv_ref, g_ref, q_ref, k_ref, v_ref, kc_ref, vc_ref,
         cq_ref, saq_ref, sbq_ref, ck_ref, sak_ref, sbk_ref, o_ref) = refs
    else:
        lamv_ref, g_ref, q_ref, k_ref, v_ref, o_ref = refs
    lv = lamv_ref[...]
    lam = (jnp.exp(jnp.sum(lv[0:1] * lv[1:2], axis=-1, keepdims=True))
           - jnp.exp(jnp.sum(lv[2:3] * lv[3:4], axis=-1, keepdims=True)) + lam_init)
    scale = DIFF_QK_DIM ** -0.5
    half = DIFF_QK_DIM // 2
    for hh in range(hb):
        cols = slice(hh * LANES, (hh + 1) * LANES)
        q = q_ref[:, cols]
        k = k_ref[:, cols]
        if latent:
            q = _apply_rope(q, cq_ref[...], saq_ref[...], sbq_ref[...], half)
            k = _apply_rope(k, ck_ref[...], sak_ref[...], sbk_ref[...], half)
        lane = lax.broadcasted_iota(jnp.int32, q.shape, 1)
        first = lane < DIFF_QK_DIM
        kb = k.astype(BF16)
        vb = v_ref[:, cols].astype(BF16)
        if latent:
            kcb = kc_ref[:, cols].astype(BF16)
            vcb = vc_ref[:, cols].astype(BF16)
        outs = []
        for m in range(2):
            qm = jnp.where(first if m == 0 else jnp.logical_not(first), q, 0.0).astype(BF16)
            scores, values = [], []
            if latent:
                scores.append(lax.dot_general(qm, kcb, _NT, preferred_element_type=F32) * scale)
                values.append(vcb)
            scores.append(lax.dot_general(qm, kb, _NT, preferred_element_type=F32) * scale)
            values.append(vb)
            outs.append(_softmax_pv(scores, values))
        o = outs[0] - lam * outs[1]
        ms = jnp.mean(o * o, axis=-1, keepdims=True)
        o = o * lax.rsqrt(ms + NORM_EPS) * g_ref[...] * (1.0 - lam_init)
        o_ref[:, cols] = o.astype(o_ref.dtype)


def _diff_attention(proj, lamv, subln_g, lam_init, layer, cache_k, cache_v, tabs64):
    g = subln_g.reshape(1, HEAD_DIM)
    hw = DIFF_HB * LANES
    small = [pl.BlockSpec((4, DIFF_QK_DIM), lambda *a: (0, 0)), pl.BlockSpec((1, HEAD_DIM), lambda *a: (0, 0))]
    hwc = DIFF_HB_CTX * LANES
    ctx = pl.pallas_call(
        functools.partial(_diff_kernel, lam_init=lam_init, latent=False, hb=DIFF_HB_CTX),
        grid=(BATCH, DIFF_HEADS // DIFF_HB_CTX),
        in_specs=small + [
            pl.BlockSpec((SEQ, hwc), lambda b, h: (b, COL_QA // DIFF_HB_CTX + h)),
            pl.BlockSpec((SEQ, hwc), lambda b, h: (b, COL_KA // DIFF_HB_CTX + h)),
            pl.BlockSpec((SEQ, hwc), lambda b, h: (b, COL_VA // DIFF_HB_CTX + h)),
        ],
        out_specs=pl.BlockSpec((SEQ, hwc), lambda b, h: (b, h)),
        out_shape=jax.ShapeDtypeStruct((N_TOK, MIX_WIDTH), BF16),
        compiler_params=_params("arbitrary", "arbitrary"),
        name="diff_ctx",
    )(lamv, g, proj, proj, proj)

    nq = DEC_SEQ // ATT_TQ
    q_row = lambda b, qi: (N_CTX // ATT_TQ) + b * nq + qi
    kv_row = lambda b: (N_CTX // DEC_SEQ) + b
    ck = cache_k.reshape(DEC_BATCH, DEPTH, PAST_LEN, DIFF_HEADS * HEAD_DIM)
    cv = cache_v.reshape(DEC_BATCH, DEPTH, PAST_LEN, DIFF_HEADS * HEAD_DIM)
    tab_q = pl.BlockSpec((ATT_TQ, LANES), lambda b, h, qi: (qi, 0))
    tab_k = pl.BlockSpec((DEC_SEQ, LANES), lambda b, h, qi: (0, 0))
    return pl.pallas_call(
        functools.partial(_diff_kernel, lam_init=lam_init, latent=True, hb=DIFF_HB),
        grid=(DEC_BATCH, DIFF_HEADS // DIFF_HB, nq),
        in_specs=[_ANY] + small + [
            pl.BlockSpec((ATT_TQ, hw), lambda b, h, qi: (q_row(b, qi), COL_QA // DIFF_HB + h)),
            pl.BlockSpec((DEC_SEQ, hw), lambda b, h, qi: (kv_row(b), COL_KA // DIFF_HB + h)),
            pl.BlockSpec((DEC_SEQ, hw), lambda b, h, qi: (kv_row(b), COL_VA // DIFF_HB + h)),
            pl.BlockSpec((None, None, PAST_LEN, hw), lambda b, h, qi: (b, layer, 0, h)),
            pl.BlockSpec((None, None, PAST_LEN, hw), lambda b, h, qi: (b, layer, 0, h)),
            tab_q, tab_q, tab_q, tab_k, tab_k, tab_k,
        ],
        out_specs=pl.BlockSpec((ATT_TQ, hw), lambda b, h, qi: (q_row(b, qi), h)),
        out_shape=jax.ShapeDtypeStruct((N_TOK, MIX_WIDTH), BF16),
        input_output_aliases={0: 0},
        compiler_params=_params("arbitrary", "arbitrary", "arbitrary"),
        name="diff_lat",
    )(ctx, lamv, g, proj, proj, proj, ck, cv, *tabs64, *tabs64)


def _head_rms(x, g):
    ms = jnp.mean(x * x, axis=-1, keepdims=True)
    return x * lax.rsqrt(ms + NORM_EPS) * g


def _gqa_kernel(*refs, latent, gb):
    refs = refs[1:]
    if latent:
        (qg_ref, kg_ref, q_ref, k_ref, v_ref, kc_ref, vc_ref,
         cq_ref, saq_ref, sbq_ref, ck_ref, sak_ref, sbk_ref, o_ref) = refs
    else:
        qg_ref, kg_ref, q_ref, k_ref, v_ref, o_ref, kn_ref = refs
    half = HEAD_DIM // 2
    gw = GQA_GROUP * HEAD_DIM
    tq = q_ref.shape[0]
    scale = HEAD_DIM ** -0.5
    for gi in range(gb):
        kcols = slice(gi * HEAD_DIM, (gi + 1) * HEAD_DIM)
        kn = _head_rms(k_ref[:, kcols], kg_ref[...])
        if latent:
            kn = _apply_rope(kn, ck_ref[...], sak_ref[...], sbk_ref[...], half)
        else:
            kn_ref[:, kcols] = kn
        heads = []
        for j in range(GQA_GROUP):
            c0 = gi * gw + j * HEAD_DIM
            qj = _head_rms(q_ref[:, c0:c0 + HEAD_DIM], qg_ref[...])
            if latent:
                qj = _apply_rope(qj, cq_ref[...], saq_ref[...], sbq_ref[...], half)
            heads.append(qj.astype(BF16))
        qs = jnp.concatenate(heads, axis=0)
        scores, values = [], []
        if latent:
            scores.append(lax.dot_general(qs, kc_ref[:, kcols].astype(BF16), _NT, preferred_element_type=F32) * scale)
            values.append(vc_ref[:, kcols].astype(BF16))
        scores.append(lax.dot_general(qs, kn.astype(BF16), _NT, preferred_element_type=F32) * scale)
        values.append(v_ref[:, kcols].astype(BF16))
        o = _softmax_pv(scores, values)
        for j in range(GQA_GROUP):
            c0 = gi * gw + j * HEAD_DIM
            o_ref[:, c0:c0 + HEAD_DIM] = o[j * tq:(j + 1) * tq].astype(o_ref.dtype)


def _gqa_attention(mix, proj, q_g, k_g, layer, cache_k, cache_v, tabs128):
    qg = q_g.reshape(1, HEAD_DIM)
    kg = k_g.reshape(1, HEAD_DIM)
    gw = GQA_GROUP * HEAD_DIM
    small = [pl.BlockSpec((1, HEAD_DIM), lambda *a: (0, 0)), pl.BlockSpec((1, HEAD_DIM), lambda *a: (0, 0))]
    qcol = COL_QB * LANES // gw
    mcol = DIFF_HEADS * HEAD_DIM // gw
    mix, kn = pl.pallas_call(
        functools.partial(_gqa_kernel, latent=False, gb=GQA_GB_CTX),
        grid=(BATCH, GQA_KV_HEADS // GQA_GB_CTX),
        in_specs=[_ANY] + small + [
            pl.BlockSpec((SEQ, gw * GQA_GB_CTX), lambda b, g: (b, qcol // GQA_GB_CTX + g)),
            pl.BlockSpec((SEQ, LANES * GQA_GB_CTX), lambda b, g: (b, COL_KB // GQA_GB_CTX + g)),
            pl.BlockSpec((SEQ, LANES * GQA_GB_CTX), lambda b, g: (b, COL_VB // GQA_GB_CTX + g)),
        ],
        out_specs=[pl.BlockSpec((SEQ, gw * GQA_GB_CTX), lambda b, g: (b, mcol // GQA_GB_CTX + g)),
                   pl.BlockSpec((SEQ, LANES * GQA_GB_CTX), lambda b, g: (b, g))],
        out_shape=[jax.ShapeDtypeStruct((N_TOK, MIX_WIDTH), BF16),
                   jax.ShapeDtypeStruct((N_CTX, GQA_KV_HEADS * HEAD_DIM), F32)],
        input_output_aliases={0: 0},
        compiler_params=_params("arbitrary", "arbitrary"),
        name="gqa_ctx",
    )(mix, qg, kg, proj, proj, proj)

    nq = DEC_SEQ // ATT_TQ
    q_row = lambda b, qi: (N_CTX // ATT_TQ) + b * nq + qi
    kv_row = lambda b: (N_CTX // DEC_SEQ) + b
    ck = cache_k.reshape(DEC_BATCH, DEPTH, PAST_LEN, GQA_KV_HEADS * HEAD_DIM)
    cv = cache_v.reshape(DEC_BATCH, DEPTH, PAST_LEN, GQA_KV_HEADS * HEAD_DIM)
    tab_q = pl.BlockSpec((ATT_TQ, LANES), lambda b, g, qi: (qi, 0))
    tab_k = pl.BlockSpec((DEC_SEQ, LANES), lambda b, g, qi: (0, 0))
    mix = pl.pallas_call(
        functools.partial(_gqa_kernel, latent=True, gb=1),
        grid=(DEC_BATCH, GQA_KV_HEADS, nq),
        in_specs=[_ANY] + small + [
            pl.BlockSpec((ATT_TQ, gw), lambda b, g, qi: (q_row(b, qi), qcol + g)),
            pl.BlockSpec((DEC_SEQ, LANES), lambda b, g, qi: (kv_row(b), COL_KB + g)),
            pl.BlockSpec((DEC_SEQ, LANES), lambda b, g, qi: (kv_row(b), COL_VB + g)),
            pl.BlockSpec((None, None, PAST_LEN, LANES), lambda b, g, qi: (b, layer, 0, g)),
            pl.BlockSpec((None, None, PAST_LEN, LANES), lambda b, g, qi: (b, layer, 0, g)),
            tab_q, tab_q, tab_q, tab_k, tab_k, tab_k,
        ],
        out_specs=pl.BlockSpec((ATT_TQ, gw), lambda b, g, qi: (q_row(b, qi), mcol + g)),
        out_shape=jax.ShapeDtypeStruct((N_TOK, MIX_WIDTH), BF16),
        input_output_aliases={0: 0},
        compiler_params=_params("arbitrary", "arbitrary", "arbitrary"),
        name="gqa_lat",
    )(mix, qg, kg, proj, proj, proj, ck, cv, *tabs128, *tabs128)
    return mix, kn


def _ret_kernel(*refs, seq, latent):
    refs = refs[1:]
    if latent:
        (lgf_ref, lgb_ref, q_ref, k_ref, v_ref, gc_ref, s0f_ref, s0b_ref,
         cq_ref, saq_ref, sbq_ref, ck_ref, sak_ref, sbk_ref, o_ref) = refs
    else:
        lgf_ref, lgb_ref, q_ref, k_ref, v_ref, gc_ref, o_ref, sf_ref, sb_ref = refs
    hp = pl.program_id(1)
    qi = pl.program_id(2)
    q = q_ref[...]
    k = k_ref[...] * (RET_QK_DIM ** -0.5)
    if latent:
        half = RET_QK_DIM // 2
        q = _apply_rope(q, cq_ref[...], saq_ref[...], sbq_ref[...], half)
        k = _apply_rope(k, ck_ref[...], sak_ref[...], sbk_ref[...], half)
    tq = q.shape[0]
    t_idx = qi * tq + lax.broadcasted_iota(jnp.int32, (tq, seq), 0)
    s_idx = lax.broadcasted_iota(jnp.int32, (tq, seq), 1)
    rel = (t_idx - s_idx).astype(F32)
    t_col = (qi * tq + lax.broadcasted_iota(jnp.int32, (tq, 1), 0)).astype(F32)
    s_col = lax.broadcasted_iota(jnp.int32, (seq, 1), 0).astype(F32)
    lane_q = lax.broadcasted_iota(jnp.int32, q.shape, 1)
    lane_k = lax.broadcasted_iota(jnp.int32, k.shape, 1)
    kb = k.astype(BF16)
    for j in range(2):
        lgf = lgf_ref[2 * hp + j]
        lgb = lgb_ref[2 * hp + j]
        decay = (jnp.where(rel >= 0, jnp.exp(jnp.maximum(rel, 0.0) * lgf), 0.0)
                 + jnp.where(rel <= 0, jnp.exp(jnp.maximum(-rel, 0.0) * lgb), 0.0))
        in_head_q = (lane_q >= j * RET_QK_DIM) & (lane_q < (j + 1) * RET_QK_DIM)
        qj = jnp.where(in_head_q, q, 0.0).astype(BF16)
        vj = v_ref[:, j * HEAD_DIM:(j + 1) * HEAD_DIM].astype(BF16)
        scores = lax.dot_general(qj, kb, _NT, preferred_element_type=F32) * decay
        o = jnp.dot(scores.astype(BF16), vj, preferred_element_type=F32)
        if latent:
            o = o + (jnp.dot(qj, s0f_ref[...].astype(BF16), preferred_element_type=F32)
                     * jnp.exp((t_col + 1.0) * lgf))
            o = o + (jnp.dot(qj, s0b_ref[...].astype(BF16), preferred_element_type=F32)
                     * jnp.exp((seq - t_col) * lgb))
        else:
            in_head_k = (lane_k >= j * RET_QK_DIM) & (lane_k < (j + 1) * RET_QK_DIM)
            kj = jnp.where(in_head_k, k, 0.0)
            kf = (kj * jnp.exp((seq - 1.0 - s_col) * lgf)).astype(BF16)
            kr = (kj * jnp.exp(s_col * lgb)).astype(BF16)
            sf = lax.dot_general(kf, vj, _TN, preferred_element_type=F32)
            sb = lax.dot_general(kr, vj, _TN, preferred_element_type=F32)
            sf_ref[0, j] = sf[j * RET_QK_DIM:(j + 1) * RET_QK_DIM]
            sb_ref[0, j] = sb[j * RET_QK_DIM:(j + 1) * RET_QK_DIM]
        mu = jnp.mean(o, axis=-1, keepdims=True)
        oc = o - mu
        var = jnp.mean(oc * oc, axis=-1, keepdims=True)
        gcj = gc_ref[:, j * HEAD_DIM:(j + 1) * HEAD_DIM]
        silu = gcj * (1.0 / (1.0 + jnp.exp(-gcj)))
        o_ref[:, j * HEAD_DIM:(j + 1) * HEAD_DIM] = (oc * lax.rsqrt(var + NORM_EPS) * silu).astype(o_ref.dtype)


def _retention(mix, proj, lg_f, lg_b, layer, state_f, state_b, tabs64):
    pairs = RET_HEADS // 2
    pw = 2 * HEAD_DIM
    smem = pl.BlockSpec(memory_space=pltpu.SMEM)
    vcol = COL_VC * LANES // pw
    gcol = COL_GC * LANES // pw
    mcol = (DIFF_HEADS + GQA_Q_HEADS) * HEAD_DIM // pw
    mix, sf, sb = pl.pallas_call(
        functools.partial(_ret_kernel, seq=SEQ, latent=False),
        grid=(BATCH, pairs, 1),
        in_specs=[
            _ANY, smem, smem,
            pl.BlockSpec((SEQ, LANES), lambda b, p, qi: (b, COL_QC + p)),
            pl.BlockSpec((SEQ, LANES), lambda b, p, qi: (b, COL_KC + p)),
            pl.BlockSpec((SEQ, pw), lambda b, p, qi: (b, vcol + p)),
            pl.BlockSpec((SEQ, pw), lambda b, p, qi: (b, gcol + p)),
        ],
        out_specs=[
            pl.BlockSpec((SEQ, pw), lambda b, p, qi: (b, mcol + p)),
            pl.BlockSpec((1, 2, RET_QK_DIM, HEAD_DIM), lambda b, p, qi: (b, p, 0, 0)),
            pl.BlockSpec((1, 2, RET_QK_DIM, HEAD_DIM), lambda b, p, qi: (b, p, 0, 0)),
        ],
        out_shape=[
            jax.ShapeDtypeStruct((N_TOK, MIX_WIDTH), BF16),
            jax.ShapeDtypeStruct((BATCH, RET_HEADS, RET_QK_DIM, HEAD_DIM), F32),
            jax.ShapeDtypeStruct((BATCH, RET_HEADS, RET_QK_DIM, HEAD_DIM), F32),
        ],
        input_output_aliases={0: 0},
        compiler_params=_params("arbitrary", "arbitrary", "arbitrary"),
        name="ret_ctx",
    )(mix, lg_f, lg_b, proj, proj, proj, proj)

    nq = DEC_SEQ // ATT_TQ
    q_row = lambda b, qi: (N_CTX // ATT_TQ) + b * nq + qi
    kv_row = lambda b: (N_CTX // DEC_SEQ) + b
    s0f = state_f.reshape(DEC_BATCH, DEPTH, pairs, 2 * RET_QK_DIM, HEAD_DIM)
    s0b = state_b.reshape(DEC_BATCH, DEPTH, pairs, 2 * RET_QK_DIM, HEAD_DIM)
    tab_q = pl.BlockSpec((ATT_TQ, LANES), lambda b, p, qi: (qi, 0))
    tab_k = pl.BlockSpec((DEC_SEQ, LANES), lambda b, p, qi: (0, 0))
    state_spec = pl.BlockSpec((None, None, None, 2 * RET_QK_DIM, HEAD_DIM), lambda b, p, qi: (b, layer, p, 0, 0))
    mix = pl.pallas_call(
        functools.partial(_ret_kernel, seq=DEC_SEQ, latent=True),
        grid=(DEC_BATCH, pairs, nq),
        in_specs=[
            _ANY, smem, smem,
            pl.BlockSpec((ATT_TQ, LANES), lambda b, p, qi: (q_row(b, qi), COL_QC + p)),
            pl.BlockSpec((DEC_SEQ, LANES), lambda b, p, qi: (kv_row(b), COL_KC + p)),
            pl.BlockSpec((DEC_SEQ, pw), lambda b, p, qi: (kv_row(b), vcol + p)),
            pl.BlockSpec((ATT_TQ, pw), lambda b, p, qi: (q_row(b, qi), gcol + p)),
            state_spec, state_spec,
            tab_q, tab_q, tab_q, tab_k, tab_k, tab_k,
        ],
        out_specs=pl.BlockSpec((ATT_TQ, pw), lambda b, p, qi: (q_row(b, qi), mcol + p)),
        out_shape=jax.ShapeDtypeStruct((N_TOK, MIX_WIDTH), BF16),
        input_output_aliases={0: 0},
        compiler_params=_params("arbitrary", "arbitrary", "arbitrary"),
        name="ret_lat",
    )(mix, lg_f, lg_b, proj, proj, proj, proj, s0f, s0b, *tabs64, *tabs64)
    return mix, sf, sb


def _router_kernel(x_ref, g_ref, sh_ref, sc_ref, rw_ref, rb_ref, h_ref, e_ref, gate_ref):
    h = _ada_norm_value(x_ref[...], g_ref[...], sh_ref[0], sc_ref[0])
    h_ref[...] = h
    hh = h.astype(BF16)
    hl = (h - hh.astype(F32)).astype(BF16)
    w = rw_ref[...]
    wh = w.astype(BF16)
    wl = (w - wh.astype(F32)).astype(BF16)
    logits = (jnp.dot(hh, wh, preferred_element_type=F32) + jnp.dot(hh, wl, preferred_element_type=F32)
              + jnp.dot(hl, wh, preferred_element_type=F32) + rb_ref[...])
    lane = lax.broadcasted_iota(jnp.int32, logits.shape, 1).astype(F32)
    wide = lax.broadcasted_iota(jnp.int32, e_ref.shape, 1)
    e_out = jnp.zeros(e_ref.shape, F32)
    v_out = jnp.zeros(e_ref.shape, F32)
    top0 = None
    den = None
    for kk in range(TOP_K):
        m = jnp.max(logits, axis=-1, keepdims=True)
        idx = jnp.min(jnp.where(logits == m, lane, float(N_EXPERTS)), axis=-1, keepdims=True)
        logits = jnp.where(lane == idx, -jnp.inf, logits)
        if kk == 0:
            top0 = m
        ex = jnp.exp(m - top0)
        den = ex if den is None else den + ex
        e_out = jnp.where(wide == kk, idx, e_out)
        v_out = jnp.where(wide == kk, ex, v_out)
    e_ref[...] = e_out.astype(jnp.int32)
    gate_ref[...] = v_out / den


def _router(x, g, modt, layer, router_w, router_b):
    tm = 256
    sh0 = _mod_index(layer, 3)
    sc0 = _mod_index(layer, 4)
    return pl.pallas_call(
        _router_kernel,
        grid=(N_TOK // tm,),
        in_specs=[
            pl.BlockSpec((tm, D_MODEL), lambda i: (i, 0)),
            pl.BlockSpec((1, D_MODEL), lambda i: (0, 0)),
            pl.BlockSpec((1, 1, D_MODEL), lambda i: (sh0 + _stream_of_row_tile(i, tm), 0, 0)),
            pl.BlockSpec((1, 1, D_MODEL), lambda i: (sc0 + _stream_of_row_tile(i, tm), 0, 0)),
            pl.BlockSpec((None, D_MODEL, N_EXPERTS), lambda i: (layer, 0, 0)),
            pl.BlockSpec((None, 1, N_EXPERTS), lambda i: (layer, 0, 0)),
        ],
        out_specs=[
            pl.BlockSpec((tm, D_MODEL), lambda i: (i, 0)),
            pl.BlockSpec((tm, LANES), lambda i: (i, 0)),
            pl.BlockSpec((tm, LANES), lambda i: (i, 0)),
        ],
        out_shape=[
            jax.ShapeDtypeStruct((N_TOK, D_MODEL), F32),
            jax.ShapeDtypeStruct((N_TOK, LANES), jnp.int32),
            jax.ShapeDtypeStruct((N_TOK, LANES), F32),
        ],
        compiler_params=_params("arbitrary"),
        name="router",
    )(x, g.reshape(1, D_MODEL), modt, modt, router_w, router_b.reshape(DEPTH, 1, N_EXPERTS))


def _route_meta(top_e):
    n_assign = N_TOK * TOP_K
    n_items = -(-n_assign // MOE_ROWS) + N_EXPERTS
    flat_e = top_e.reshape(n_assign)
    order = jnp.argsort(flat_e).astype(jnp.int32)
    inv = jnp.argsort(order).astype(jnp.int32)
    onehot = (flat_e[:, None] == jnp.arange(N_EXPERTS, dtype=jnp.int32)[None, :]).astype(jnp.int32)
    counts = jnp.sum(onehot, axis=0)
    padded = (counts + MOE_ROWS - 1) // MOE_ROWS * MOE_ROWS
    start = jnp.cumsum(counts) - counts
    pad_end = jnp.cumsum(padded)
    pad_start = pad_end - padded
    slot_of = inv + jnp.sum(onehot * (pad_start - start)[None, :], axis=1)
    tok_sorted = jnp.concatenate([order // TOP_K, jnp.zeros((8,), jnp.int32)])
    item_start = jnp.arange(n_items, dtype=jnp.int32) * MOE_ROWS
    item_e = jnp.minimum(jnp.searchsorted(pad_end, item_start, side='right'), N_EXPERTS - 1).astype(jnp.int32)
    off = item_start - pad_start[item_e]
    item_rows = jnp.clip(counts[item_e] - off, 0, MOE_ROWS).astype(jnp.int32)
    item_rows8 = (item_rows + 7) // 8 * 8
    item_src = jnp.clip(start[item_e] + off, 0, n_assign - 1).astype(jnp.int32)
    n_act = (pad_end[-1] // MOE_ROWS).astype(jnp.int32).reshape(1)
    return tok_sorted, slot_of.astype(jnp.int32), (item_e, item_rows, item_rows8, item_src, n_act), n_items


def _expert_kernel(ie_ref, rows_ref, rows8_ref, src_ref, nact_ref, tok_ref,
                   h_hbm, wg_ref, wu_ref, bg_ref, bu_ref, wd_ref, bd_ref, o_ref, x_ref, act_ref, sem):
    it = pl.program_id(0)
    s = pl.program_id(1)
    n_act = nact_ref[0]
    active = it < n_act
    rows = rows_ref[it]
    n_full = rows // MOE_SUB
    rem = rows - n_full * MOE_SUB
    n_main = n_full + (rem > MOE_SUB // 2).astype(jnp.int32)
    tail0 = pl.multiple_of(n_full * MOE_SUB, MOE_SUB)
    short_tail = (rem > 0) & (rem <= MOE_SUB // 4)
    half_tail = (rem > MOE_SUB // 4) & (rem <= MOE_SUB // 2)

    def row_copy(tok, r):
        return pltpu.make_async_copy(h_hbm.at[pl.ds(tok, 1), :], x_ref.at[pl.ds(r, 1), :], sem.at[0])

    def issue(item, lo, hi):
        base = src_ref[item]

        def body(g, carry):
            r0 = pl.multiple_of(g * 8, 8)
            for u in range(8):
                row_copy(tok_ref[base + r0 + u], r0 + u).start()
            return carry
        lax.fori_loop(lo // 8, hi // 8, body, 0)

    @pl.when(active & (s == 0))
    def _():
        @pl.when(it == 0)
        def _():
            x_ref[...] = jnp.zeros(x_ref.shape, F32)
            issue(0, 0, rows8_ref[0])

        def wait_body(g, carry):
            for u in range(8):
                row_copy(0, 0).wait()
            return carry
        lax.fori_loop(0, rows8_ref[it] // 8, wait_body, 0)

    def gate_up(r0, m):
        xs = x_ref[pl.ds(r0, m), :].astype(BF16)
        gate = jnp.dot(xs, wg_ref[...].astype(BF16), preferred_element_type=F32) + bg_ref[...]
        up = jnp.dot(xs, wu_ref[...].astype(BF16), preferred_element_type=F32) + bu_ref[...]
        gate = jnp.minimum(gate, SWIGLU_LIMIT)
        up = jnp.clip(up, -SWIGLU_LIMIT, SWIGLU_LIMIT)
        a = (up + 1.0) * gate * (1.0 / (1.0 + jnp.exp(-SWIGLU_ALPHA * gate)))
        act_ref[s, pl.ds(r0, m), :] = a.astype(BF16)

    def down(r0, m):
        acc = None
        for f in range(MOE_NF):
            d = jnp.dot(act_ref[f, pl.ds(r0, m), :], wd_ref[f * MOE_TF:(f + 1) * MOE_TF, :].astype(BF16),
                        preferred_element_type=F32)
            acc = d if acc is None else acc + d
        o_ref[pl.ds(r0, m), :] = acc + bd_ref[...]

    def run_rows(fn):
        def body(i, carry):
            fn(pl.multiple_of(i * 2 * MOE_SUB, 2 * MOE_SUB), 2 * MOE_SUB)
            return carry
        lax.fori_loop(0, n_main // 2, body, 0)

        @pl.when(n_main % 2 == 1)
        def _():
            fn(pl.multiple_of((n_main - 1) * MOE_SUB, MOE_SUB), MOE_SUB)

        @pl.when(short_tail)
        def _():
            fn(tail0, MOE_SUB // 4)

        @pl.when(half_tail)
        def _():
            fn(tail0, MOE_SUB // 2)

    @pl.when(active & (s < MOE_NF))
    def _():
        run_rows(gate_up)

    @pl.when(active & (s >= MOE_NF))
    def _():
        @pl.when(it + 1 < n_act)
        def _():
            d = s - MOE_NF
            issue(it + 1, d * MOE_ISSUE, jnp.minimum((d + 1) * MOE_ISSUE, rows8_ref[it + 1]))
        run_rows(down)


def _experts(h, tok_sorted, item_meta, n_items, layer, w_gu, b_gu, w_dn, b_dn):
    n_steps = MOE_NF + MOE_ND
    cap = n_items * MOE_ROWS

    def pos(it, s, ie, nact):
        last = jnp.maximum(nact[0] - 1, 0)
        itc = jnp.minimum(it, last)
        sc = jnp.where(it < nact[0], s, n_steps - 1)
        return itc, sc, ie[itc]

    def wg_map(it, s, ie, rows, rows8, src, nact, tok):
        _, sc, e = pos(it, s, ie, nact)
        return (layer, e, 0, jnp.minimum(sc, MOE_NF - 1))

    def wu_map(it, s, ie, rows, rows8, src, nact, tok):
        _, sc, e = pos(it, s, ie, nact)
        return (layer, e, 0, MOE_NF + jnp.minimum(sc, MOE_NF - 1))

    def wd_map(it, s, ie, rows, rows8, src, nact, tok):
        _, sc, e = pos(it, s, ie, nact)
        return (layer, e, 0, jnp.clip(sc - MOE_NF, 0, MOE_ND - 1))

    def o_map(it, s, ie, rows, rows8, src, nact, tok):
        itc, sc, _ = pos(it, s, ie, nact)
        return (itc, jnp.clip(sc - MOE_NF, 0, MOE_ND - 1))

    grid_spec = pltpu.PrefetchScalarGridSpec(
        num_scalar_prefetch=6,
        grid=(n_items, n_steps),
        in_specs=[
            _ANY,
            pl.BlockSpec((None, None, D_MODEL, MOE_TF), wg_map),
            pl.BlockSpec((None, None, D_MODEL, MOE_TF), wu_map),
            pl.BlockSpec((None, None, 1, MOE_TF), wg_map),
            pl.BlockSpec((None, None, 1, MOE_TF), wu_map),
            pl.BlockSpec((None, None, EXPERT_FF, MOE_TD), wd_map),
            pl.BlockSpec((None, None, 1, MOE_TD), wd_map),
        ],
        out_specs=pl.BlockSpec((MOE_ROWS, MOE_TD), o_map),
        scratch_shapes=[
            pltpu.VMEM((MOE_ROWS, D_MODEL), F32),
            pltpu.VMEM((MOE_NF, MOE_ROWS, MOE_TF), BF16),
            pltpu.SemaphoreType.DMA((1,)),
        ],
    )
    b_gu4 = b_gu.reshape(DEPTH, N_EXPERTS, 1, 2 * EXPERT_FF)
    return pl.pallas_call(
        _expert_kernel,
        grid_spec=grid_spec,
        out_shape=jax.ShapeDtypeStruct((cap, D_MODEL), F32),
        compiler_params=_params("arbitrary", "arbitrary"),
        name="moe_experts",
    )(*item_meta, tok_sorted, h, w_gu, w_gu, b_gu4, b_gu4, w_dn, b_dn.reshape(DEPTH, N_EXPERTS, 1, D_MODEL))


def _combine_kernel(slot_ref, y_hbm, x_ref, gates_ref, g2_ref, fg_ref, o_ref, buf, sem, *, final_norm):
    i = pl.program_id(0)
    n = pl.num_programs(0)

    def copy(tile, slot, r, kk):
        row = slot_ref[(tile * COMBINE_TOK + r) * TOP_K + kk]
        return pltpu.make_async_copy(y_hbm.at[pl.ds(row, 1), :], buf.at[slot, kk, pl.ds(r, 1), :], sem.at[slot])

    def issue(tile, slot):
        def body(r, carry):
            for kk in range(TOP_K):
                copy(tile, slot, r, kk).start()
            return carry
        lax.fori_loop(0, COMBINE_TOK, body, 0)

    @pl.when(i == 0)
    def _():
        issue(0, 0)

    @pl.when(i + 1 < n)
    def _():
        issue(i + 1, (i + 1) % 2)

    slot = i % 2

    def wait_body(r, carry):
        for kk in range(TOP_K):
            copy(i, slot, r, kk).wait()
        return carry
    lax.fori_loop(0, COMBINE_TOK, wait_body, 0)

    gates = gates_ref[...]
    y = gates[:, 0:1] * buf[slot, 0]
    for kk in range(1, TOP_K):
        y = y + gates[:, kk:kk + 1] * buf[slot, kk]
    out = x_ref[...] + g2_ref[0] * y
    if final_norm:
        ms = jnp.mean(out * out, axis=-1, keepdims=True)
        out = out * lax.rsqrt(ms + NORM_EPS) * fg_ref[...]
    o_ref[...] = out


def _combine(yb, slot_of, x, gates, modt, layer, final_g, final_norm):
    tc = COMBINE_TOK
    g0 = _mod_index(layer, 5)
    grid_spec = pltpu.PrefetchScalarGridSpec(
        num_scalar_prefetch=1,
        grid=(N_TOK // tc,),
        in_specs=[
            pl.BlockSpec(memory_space=pl.ANY),
            pl.BlockSpec((tc, D_MODEL), lambda i, sl: (i, 0)),
            pl.BlockSpec((tc, LANES), lambda i, sl: (i, 0)),
            pl.BlockSpec((1, 1, D_MODEL), lambda i, sl: (g0 + _stream_of_row_tile(i, tc), 0, 0)),
            pl.BlockSpec((1, D_MODEL), lambda i, sl: (0, 0)),
        ],
        out_specs=pl.BlockSpec((tc, D_MODEL), lambda i, sl: (i, 0)),
        scratch_shapes=[pltpu.VMEM((2, TOP_K, tc, D_MODEL), F32), pltpu.SemaphoreType.DMA((2,))],
    )
    return pl.pallas_call(
        functools.partial(_combine_kernel, final_norm=final_norm),
        grid_spec=grid_spec,
        out_shape=jax.ShapeDtypeStruct((N_TOK, D_MODEL), F32),
        compiler_params=_params("arbitrary"),
        name="moe_combine",
    )(slot_of, yb, x, gates, modt, final_g.reshape(1, D_MODEL))


def kernel(x_prompt, x_sample, cache_diff_k, cache_diff_v, cache_gqa_k, cache_gqa_v, state_ret_fwd, state_ret_bwd, c, c_ctx, w_mod, b_mod, norm_mix_g, norm_ffn_g, w_in, w_out, diff_lambda, diff_subln_g, gqa_q_norm_g, gqa_k_norm_g, ret_log_decay_fwd, ret_log_decay_bwd, router_w, router_b, moe_w_gate_up, moe_b_gate_up, moe_w_down, moe_b_down, final_norm_g):
    x = jnp.concatenate([x_prompt.reshape(N_CTX, D_MODEL), x_sample.reshape(N_LAT, D_MODEL)], axis=0)
    cond_raw = jnp.concatenate([c_ctx[None].astype(F32), c.astype(F32),
                                jnp.zeros((8 - N_STREAM, D_MODEL), F32)], axis=0)
    modt = _modulation(cond_raw, w_mod, b_mod)
    tabs64 = _rope_tables(DEC_SEQ, DIFF_QK_DIM)
    tabs128 = _rope_tables(DEC_SEQ, HEAD_DIM)

    new_dk, new_dv, new_gk, new_gv, new_sf, new_sb = [], [], [], [], [], []
    for layer in range(DEPTH):
        lam_init = 0.8 - 0.6 * math.exp(-0.3 * layer)
        h = _adanorm(x, norm_mix_g[layer], modt, layer, 0, 1)
        proj = _in_proj(h, w_in[layer])
        mix = _diff_attention(proj, diff_lambda[layer], diff_subln_g[layer], lam_init, layer,
                              cache_diff_k, cache_diff_v, tabs64)
        mix, kn = _gqa_attention(mix, proj, gqa_q_norm_g[layer], gqa_k_norm_g[layer], layer,
                                 cache_gqa_k, cache_gqa_v, tabs128)
        mix, sf, sb = _retention(mix, proj, ret_log_decay_fwd[layer], ret_log_decay_bwd[layer], layer,
                                 state_ret_fwd, state_ret_bwd, tabs64)
        x = _out_proj(mix, w_out[layer], x, modt, layer, 2)

        h2, top_e, gates = _router(x, norm_ffn_g[layer], modt, layer, router_w, router_b)
        tok_sorted, slot_of, item_meta, n_items = _route_meta(top_e[:, :TOP_K])
        yb = _experts(h2, tok_sorted, item_meta, n_items, layer,
                      moe_w_gate_up, moe_b_gate_up, moe_w_down, moe_b_down)
        x = _combine(yb, slot_of, x, gates, modt, layer, final_norm_g, layer == DEPTH - 1)

        pc = proj[:N_CTX]
        new_dk.append(pc[:, COL_KA * LANES:COL_VA * LANES].reshape(BATCH, SEQ, DIFF_HEADS, 2, DIFF_QK_DIM))
        new_dv.append(pc[:, COL_VA * LANES:COL_QB * LANES].reshape(BATCH, SEQ, DIFF_HEADS, HEAD_DIM))
        new_gk.append(kn.reshape(BATCH, SEQ, GQA_KV_HEADS, HEAD_DIM))
        new_gv.append(pc[:, COL_VB * LANES:COL_QC * LANES].reshape(BATCH, SEQ, GQA_KV_HEADS, HEAD_DIM))
        new_sf.append(sf)
        new_sb.append(sb)

    y_prompt = x[:N_CTX].reshape(BATCH, SEQ, D_MODEL)
    y_sample = x[N_CTX:].reshape(DEC_BATCH, DEC_SEQ, D_MODEL)
    return (y_prompt, y_sample, jnp.stack(new_dk, axis=1), jnp.stack(new_dv, axis=1),
            jnp.stack(new_gk, axis=1), jnp.stack(new_gv, axis=1),
            jnp.stack(new_sf, axis=1), jnp.stack(new_sb, axis=1))
```

```python
import functools
import math

import jax
import jax.numpy as jnp
from jax import lax
from jax.experimental import pallas as pl
from jax.experimental.pallas import tpu as pltpu

F32 = jnp.float32
BF16 = jnp.bfloat16

D_MODEL = 4096
BATCH = 16
SEQ = 256
DEPTH = 2
DEC_BATCH = 2
DEC_SEQ = 1024
PAST_LEN = 512
GRID_W = 64
HEAD_DIM = 128
DIFF_HEADS = 8
DIFF_QK_DIM = 64
GQA_Q_HEADS = 16
GQA_KV_HEADS = 4
GQA_GROUP = 4
RET_HEADS = 8
RET_QK_DIM = 64
N_EXPERTS = 32
TOP_K = 4
EXPERT_FF = D_MODEL // 2
SWIGLU_LIMIT = 7.0
SWIGLU_ALPHA = 1.702
ROPE_THETA = 10000.0
NORM_EPS = 1e-6
N_MOD = 6

N_CTX = BATCH * SEQ
N_LAT = DEC_BATCH * DEC_SEQ
N_TOK = N_CTX + N_LAT
N_STREAM = 1 + DEC_BATCH
IN_WIDTH = 9216
MIX_WIDTH = 4096

LANES = 128
COL_QA, COL_KA, COL_VA, COL_QB, COL_KB, COL_VB, COL_QC, COL_KC, COL_VC, COL_GC = 0, 8, 16, 24, 40, 44, 48, 52, 56, 64

ATT_TQ = 256
DIFF_HB = 2
DIFF_HB_CTX = 4
GQA_GB_CTX = 2
MOE_ROWS = 1024
MOE_SUB = 256
MOE_TF = 256
MOE_TD = 512
MOE_NF = EXPERT_FF // MOE_TF
MOE_ND = D_MODEL // MOE_TD
MOE_ISSUE = MOE_ROWS // MOE_ND
COMBINE_TOK = 128
VMEM_LIMIT = 60 * 1024 * 1024


def _params(*sem):
    return pltpu.CompilerParams(dimension_semantics=sem, vmem_limit_bytes=VMEM_LIMIT)


def _stream_of_row_tile(i, tm):
    r = i * tm
    return jnp.where(r < N_CTX, 0, 1 + (r - N_CTX) // DEC_SEQ)


def _mod_index(layer, which):
    return (layer * N_MOD + which) * N_STREAM


def _mod_kernel(c_ref, w_ref, b_ref, o_ref):
    c = c_ref[...]
    cond = c * (1.0 / (1.0 + jnp.exp(-c)))
    o_ref[0] = jnp.dot(cond.astype(BF16), w_ref[0].astype(BF16), preferred_element_type=F32) + b_ref[0]


def _modulation(cond_raw, w_mod, b_mod):
    tn = 1024
    n_out = N_MOD * D_MODEL
    out = pl.pallas_call(
        _mod_kernel,
        grid=(DEPTH, n_out // tn),
        in_specs=[
            pl.BlockSpec((8, D_MODEL), lambda l, j: (0, 0)),
            pl.BlockSpec((1, D_MODEL, tn), lambda l, j: (l, 0, j)),
            pl.BlockSpec((1, 1, tn), lambda l, j: (l, 0, j)),
        ],
        out_specs=pl.BlockSpec((1, 8, tn), lambda l, j: (l, 0, j)),
        out_shape=jax.ShapeDtypeStruct((DEPTH, 8, n_out), F32),
        compiler_params=_params("arbitrary", "arbitrary"),
        name="modulation",
    )(cond_raw, w_mod, b_mod.reshape(DEPTH, 1, n_out))
    out = out.reshape(DEPTH, 8, N_MOD, D_MODEL)[:, :N_STREAM]
    return out.transpose(0, 2, 1, 3).reshape(DEPTH * N_MOD * N_STREAM, 1, D_MODEL)


def _ada_norm_value(x, g, shift, scale):
    ms = jnp.mean(x * x, axis=-1, keepdims=True)
    y = x * lax.rsqrt(ms + NORM_EPS) * g
    return y * (1.0 + scale) + shift


def _adanorm_kernel(x_ref, g_ref, sh_ref, sc_ref, o_ref):
    o_ref[...] = _ada_norm_value(x_ref[...], g_ref[...], sh_ref[0], sc_ref[0]).astype(o_ref.dtype)


def _adanorm(x, g, modt, layer, which_shift, which_scale):
    tm = 256
    sh0 = _mod_index(layer, which_shift)
    sc0 = _mod_index(layer, which_scale)
    return pl.pallas_call(
        _adanorm_kernel,
        grid=(N_TOK // tm,),
        in_specs=[
            pl.BlockSpec((tm, D_MODEL), lambda i: (i, 0)),
            pl.BlockSpec((1, D_MODEL), lambda i: (0, 0)),
            pl.BlockSpec((1, 1, D_MODEL), lambda i: (sh0 + _stream_of_row_tile(i, tm), 0, 0)),
            pl.BlockSpec((1, 1, D_MODEL), lambda i: (sc0 + _stream_of_row_tile(i, tm), 0, 0)),
        ],
        out_specs=pl.BlockSpec((tm, D_MODEL), lambda i: (i, 0)),
        out_shape=jax.ShapeDtypeStruct((N_TOK, D_MODEL), BF16),
        compiler_params=_params("arbitrary"),
        name="adanorm",
    )(x, g.reshape(1, D_MODEL), modt, modt)


def _mm_kernel(x_ref, w_ref, o_ref, wb_ref):
    @pl.when(pl.program_id(1) == 0)
    def _():
        wb_ref[...] = w_ref[...].astype(BF16)

    o_ref[...] = jnp.dot(x_ref[...], wb_ref[...], preferred_element_type=F32)


def _mm_residual_kernel(x_ref, w_ref, res_ref, gate_ref, o_ref, wb_ref):
    @pl.when(pl.program_id(1) == 0)
    def _():
        wb_ref[...] = w_ref[...].astype(BF16)

    acc = jnp.dot(x_ref[...], wb_ref[...], preferred_element_type=F32)
    o_ref[...] = res_ref[...] + gate_ref[0] * acc


def _in_proj(h, w):
    tm, tn = 1536, 512
    k = h.shape[1]
    n = w.shape[1]
    return pl.pallas_call(
        _mm_kernel,
        grid=(n // tn, N_TOK // tm),
        in_specs=[
            pl.BlockSpec((tm, k), lambda j, i: (i, 0)),
            pl.BlockSpec((k, tn), lambda j, i: (0, j)),
        ],
        out_specs=pl.BlockSpec((tm, tn), lambda j, i: (i, j)),
        out_shape=jax.ShapeDtypeStruct((N_TOK, n), F32),
        scratch_shapes=[pltpu.VMEM((k, tn), BF16)],
        compiler_params=_params("arbitrary", "arbitrary"),
        name="in_proj",
    )(h, w)


def _out_proj(mix, w, res, modt, layer, which_gate):
    tm, tn = 1024, 512
    k = mix.shape[1]
    n = w.shape[1]
    g0 = _mod_index(layer, which_gate)
    return pl.pallas_call(
        _mm_residual_kernel,
        grid=(n // tn, N_TOK // tm),
        in_specs=[
            pl.BlockSpec((tm, k), lambda j, i: (i, 0)),
            pl.BlockSpec((k, tn), lambda j, i: (0, j)),
            pl.BlockSpec((tm, tn), lambda j, i: (i, j)),
            pl.BlockSpec((1, 1, tn), lambda j, i: (g0 + _stream_of_row_tile(i, tm), 0, j)),
        ],
        out_specs=pl.BlockSpec((tm, tn), lambda j, i: (i, j)),
        out_shape=jax.ShapeDtypeStruct((N_TOK, n), F32),
        scratch_shapes=[pltpu.VMEM((k, tn), BF16)],
        compiler_params=_params("arbitrary", "arbitrary"),
        name="out_proj",
    )(mix, w, res, modt)


def _rope_tables(n_tokens, unit):
    rows = n_tokens // GRID_W
    row = jnp.repeat(jnp.arange(rows), GRID_W).astype(F32)
    col = jnp.tile(jnp.arange(GRID_W), rows).astype(F32)
    n_freq = unit // 4
    inv_freq = jnp.power(ROPE_THETA, -jnp.arange(n_freq, dtype=F32) / n_freq)
    ang = jnp.concatenate([row[:, None] * inv_freq, col[:, None] * inv_freq], axis=-1)
    cos, sin = jnp.cos(ang), jnp.sin(ang)
    zero = jnp.zeros_like(sin)
    reps = LANES // unit
    c = jnp.tile(jnp.concatenate([cos, cos], axis=-1), (1, reps))
    sa = jnp.tile(jnp.concatenate([-sin, zero], axis=-1), (1, reps))
    sb = jnp.tile(jnp.concatenate([zero, sin], axis=-1), (1, reps))
    return c, sa, sb


def _apply_rope(x, c, sa, sb, half):
    return x * c + pltpu.roll(x, LANES - half, 1) * sa + pltpu.roll(x, half, 1) * sb


_ANY = pl.BlockSpec(memory_space=pl.ANY)
_NT = (((1,), (1,)), ((), ()))
_TN = (((0,), (0,)), ((), ()))


def _softmax_pv(score_list, value_list):
    m = score_list[0].max(axis=-1, keepdims=True)
    for s in score_list[1:]:
        m = jnp.maximum(m, s.max(axis=-1, keepdims=True))
    acc = None
    den = None
    for s, v in zip(score_list, value_list):
        p = jnp.exp(s - m)
        d = p.sum(axis=-1, keepdims=True)
        o = jnp.dot(p.astype(BF16), v, preferred_element_type=F32)
        acc = o if acc is None else acc + o
        den = d if den is None else den + d
    return acc / den


def _diff_kernel(*refs, lam_init, latent, hb):
    if latent:
        refs = refs[1:]
        (lamv_ref, g_ref, q_ref, k_ref, v_ref, kc_ref, vc_ref,
         cq_ref, saq_ref, sbq_ref, ck_ref, sak_ref, sbk_ref, o_ref) = refs
    else:
        lamv_ref, g_ref, q_ref, k_ref, v_ref, o_ref = refs
    lv = lamv_ref[...]
    lam = (jnp.exp(jnp.sum(lv[0:1] * lv[1:2], axis=-1, keepdims=True))
           - jnp.exp(jnp.sum(lv[2:3] * lv[3:4], axis=-1, keepdims=True)) + lam_init)
    scale = DIFF_QK_DIM ** -0.5
    half = DIFF_QK_DIM // 2
    for hh in range(hb):
        cols = slice(hh * LANES, (hh + 1) * LANES)
        q = q_ref[:, cols]
        k = k_ref[:, cols]
        if latent:
            q = _apply_rope(q, cq_ref[...], saq_ref[...], sbq_ref[...], half)
            k = _apply_rope(k, ck_ref[...], sak_ref[...], sbk_ref[...], half)
        lane = lax.broadcasted_iota(jnp.int32, q.shape, 1)
        first = lane < DIFF_QK_DIM
        kb = k.astype(BF16)
        vb = v_ref[:, cols].astype(BF16)
        if latent:
            kcb = kc_ref[:, cols].astype(BF16)
            vcb = vc_ref[:, cols].astype(BF16)
        outs = []
        for m in range(2):
            qm = jnp.where(first if m == 0 else jnp.logical_not(first), q, 0.0).astype(BF16)
            scores, values = [], []
            if latent:
                scores.append(lax.dot_general(qm, kcb, _NT, preferred_element_type=F32) * scale)
                values.append(vcb)
            scores.append(lax.dot_general(qm, kb, _NT, preferred_element_type=F32) * scale)
            values.append(vb)
            outs.append(_softmax_pv(scores, values))
        o = outs[0] - lam * outs[1]
        ms = jnp.mean(o * o, axis=-1, keepdims=True)
        o = o * lax.rsqrt(ms + NORM_EPS) * g_ref[...] * (1.0 - lam_init)
        o_ref[:, cols] = o.astype(o_ref.dtype)


def _diff_attention(proj, lamv, subln_g, lam_init, layer, cache_k, cache_v, tabs64):
    g = subln_g.reshape(1, HEAD_DIM)
    hw = DIFF_HB * LANES
    small = [pl.BlockSpec((4, DIFF_QK_DIM), lambda *a: (0, 0)), pl.BlockSpec((1, HEAD_DIM), lambda *a: (0, 0))]
    hwc = DIFF_HB_CTX * LANES
    ctx = pl.pallas_call(
        functools.partial(_diff_kernel, lam_init=lam_init, latent=False, hb=DIFF_HB_CTX),
        grid=(BATCH, DIFF_HEADS // DIFF_HB_CTX),
        in_specs=small + [
            pl.BlockSpec((SEQ, hwc), lambda b, h: (b, COL_QA // DIFF_HB_CTX + h)),
            pl.BlockSpec((SEQ, hwc), lambda b, h: (b, COL_KA // DIFF_HB_CTX + h)),
            pl.BlockSpec((SEQ, hwc), lambda b, h: (b, COL_VA // DIFF_HB_CTX + h)),
        ],
        out_specs=pl.BlockSpec((SEQ, hwc), lambda b, h: (b, h)),
        out_shape=jax.ShapeDtypeStruct((N_TOK, MIX_WIDTH), BF16),
        compiler_params=_params("arbitrary", "arbitrary"),
        name="diff_ctx",
    )(lamv, g, proj, proj, proj)

    nq = DEC_SEQ // ATT_TQ
    q_row = lambda b, qi: (N_CTX // ATT_TQ) + b * nq + qi
    kv_row = lambda b: (N_CTX // DEC_SEQ) + b
    ck = cache_k.reshape(DEC_BATCH, DEPTH, PAST_LEN, DIFF_HEADS * HEAD_DIM)
    cv = cache_v.reshape(DEC_BATCH, DEPTH, PAST_LEN, DIFF_HEADS * HEAD_DIM)
    tab_q = pl.BlockSpec((ATT_TQ, LANES), lambda b, h, qi: (qi, 0))
    tab_k = pl.BlockSpec((DEC_SEQ, LANES), lambda b, h, qi: (0, 0))
    return pl.pallas_call(
        functools.partial(_diff_kernel, lam_init=lam_init, latent=True, hb=DIFF_HB),
        grid=(DEC_BATCH, DIFF_HEADS // DIFF_HB, nq),
        in_specs=[_ANY] + small + [
            pl.BlockSpec((ATT_TQ, hw), lambda b, h, qi: (q_row(b, qi), COL_QA // DIFF_HB + h)),
            pl.BlockSpec((DEC_SEQ, hw), lambda b, h, qi: (kv_row(b), COL_KA // DIFF_HB + h)),
            pl.BlockSpec((DEC_SEQ, hw), lambda b, h, qi: (kv_row(b), COL_VA // DIFF_HB + h)),
            pl.BlockSpec((None, None, PAST_LEN, hw), lambda b, h, qi: (b, layer, 0, h)),
            pl.BlockSpec((None, None, PAST_LEN, hw), lambda b, h, qi: (b, layer, 0, h)),
            tab_q, tab_q, tab_q, tab_k, tab_k, tab_k,
        ],
        out_specs=pl.BlockSpec((ATT_TQ, hw), lambda b, h, qi: (q_row(b, qi), h)),
        out_shape=jax.ShapeDtypeStruct((N_TOK, MIX_WIDTH), BF16),
        input_output_aliases={0: 0},
        compiler_params=_params("arbitrary", "arbitrary", "arbitrary"),
        name="diff_lat",
    )(ctx, lamv, g, proj, proj, proj, ck, cv, *tabs64, *tabs64)


def _head_rms(x, g):
    ms = jnp.mean(x * x, axis=-1, keepdims=True)
    return x * lax.rsqrt(ms + NORM_EPS) * g


def _gqa_kernel(*refs, latent, gb):
    refs = refs[1:]
    if latent:
        (qg_ref, kg_ref, q_ref, k_ref, v_ref, kc_ref, vc_ref,
         cq_ref, saq_ref, sbq_ref, ck_ref, sak_ref, sbk_ref, o_ref) = refs
    else:
        qg_ref, kg_ref, q_ref, k_ref, v_ref, o_ref, kn_ref = refs
    half = HEAD_DIM // 2
    gw = GQA_GROUP * HEAD_DIM
    tq = q_ref.shape[0]
    scale = HEAD_DIM ** -0.5
    for gi in range(gb):
        kcols = slice(gi * HEAD_DIM, (gi + 1) * HEAD_DIM)
        kn = _head_rms(k_ref[:, kcols], kg_ref[...])
        if latent:
            kn = _apply_rope(kn, ck_ref[...], sak_ref[...], sbk_ref[...], half)
        else:
            kn_ref[:, kcols] = kn
        heads = []
        for j in range(GQA_GROUP):
            c0 = gi * gw + j * HEAD_DIM
            qj = _head_rms(q_ref[:, c0:c0 + HEAD_DIM], qg_ref[...])
            if latent:
                qj = _apply_rope(qj, cq_ref[...], saq_ref[...], sbq_ref[...], half)
            heads.append(qj.astype(BF16))
        qs = jnp.concatenate(heads, axis=0)
        scores, values = [], []
        if latent:
            scores.append(lax.dot_general(qs, kc_ref[:, kcols].astype(BF16), _NT, preferred_element_type=F32) * scale)
            values.append(vc_ref[:, kcols].astype(BF16))
        scores.append(lax.dot_general(qs, kn.astype(BF16), _NT, preferred_element_type=F32) * scale)
        values.append(v_ref[:, kcols].astype(BF16))
        o = _softmax_pv(scores, values)
        for j in range(GQA_GROUP):
            c0 = gi * gw + j * HEAD_DIM
            o_ref[:, c0:c0 + HEAD_DIM] = o[j * tq:(j + 1) * tq].astype(o_ref.dtype)


def _gqa_attention(mix, proj, q_g, k_g, layer, cache_k, cache_v, tabs128):
    qg = q_g.reshape(1, HEAD_DIM)
    kg = k_g.reshape(1, HEAD_DIM)
    gw = GQA_GROUP * HEAD_DIM
    small = [pl.BlockSpec((1, HEAD_DIM), lambda *a: (0, 0)), pl.BlockSpec((1, HEAD_DIM), lambda *a: (0, 0))]
    qcol = COL_QB * LANES // gw
    mcol = DIFF_HEADS * HEAD_DIM // gw
    mix, kn = pl.pallas_call(
        functools.partial(_gqa_kernel, latent=False, gb=GQA_GB_CTX),
        grid=(BATCH, GQA_KV_HEADS // GQA_GB_CTX),
        in_specs=[_ANY] + small + [
            pl.BlockSpec((SEQ, gw * GQA_GB_CTX), lambda b, g: (b, qcol // GQA_GB_CTX + g)),
            pl.BlockSpec((SEQ, LANES * GQA_GB_CTX), lambda b, g: (b, COL_KB // GQA_GB_CTX + g)),
            pl.BlockSpec((SEQ, LANES * GQA_GB_CTX), lambda b, g: (b, COL_VB // GQA_GB_CTX + g)),
        ],
        out_specs=[pl.BlockSpec((SEQ, gw * GQA_GB_CTX), lambda b, g: (b, mcol // GQA_GB_CTX + g)),
                   pl.BlockSpec((SEQ, LANES * GQA_GB_CTX), lambda b, g: (b, g))],
        out_shape=[jax.ShapeDtypeStruct((N_TOK, MIX_WIDTH), BF16),
                   jax.ShapeDtypeStruct((N_CTX, GQA_KV_HEADS * HEAD_DIM), F32)],
        input_output_aliases={0: 0},
        compiler_params=_params("arbitrary", "arbitrary"),
        name="gqa_ctx",
    )(mix, qg, kg, proj, proj, proj)

    nq = DEC_SEQ // ATT_TQ
    q_row = lambda b, qi: (N_CTX // ATT_TQ) + b * nq + qi
    kv_row = lambda b: (N_CTX // DEC_SEQ) + b
    ck = cache_k.reshape(DEC_BATCH, DEPTH, PAST_LEN, GQA_KV_HEADS * HEAD_DIM)
    cv = cache_v.reshape(DEC_BATCH, DEPTH, PAST_LEN, GQA_KV_HEADS * HEAD_DIM)
    tab_q = pl.BlockSpec((ATT_TQ, LANES), lambda b, g, qi: (qi, 0))
    tab_k = pl.BlockSpec((DEC_SEQ, LANES), lambda b, g, qi: (0, 0))
    mix = pl.pallas_call(
        functools.partial(_gqa_kernel, latent=True, gb=1),
        grid=(DEC_BATCH, GQA_KV_HEADS, nq),
        in_specs=[_ANY] + small + [
            pl.BlockSpec((ATT_TQ, gw), lambda b, g, qi: (q_row(b, qi), qcol + g)),
            pl.BlockSpec((DEC_SEQ, LANES), lambda b, g, qi: (kv_row(b), COL_KB + g)),
            pl.BlockSpec((DEC_SEQ, LANES), lambda b, g, qi: (kv_row(b), COL_VB + g)),
            pl.BlockSpec((None, None, PAST_LEN, LANES), lambda b, g, qi: (b, layer, 0, g)),
            pl.BlockSpec((None, None, PAST_LEN, LANES), lambda b, g, qi: (b, layer, 0, g)),
            tab_q, tab_q, tab_q, tab_k, tab_k, tab_k,
        ],
        out_specs=pl.BlockSpec((ATT_TQ, gw), lambda b, g, qi: (q_row(b, qi), mcol + g)),
        out_shape=jax.ShapeDtypeStruct((N_TOK, MIX_WIDTH), BF16),
        input_output_aliases={0: 0},
        compiler_params=_params("arbitrary", "arbitrary", "arbitrary"),
        name="gqa_lat",
    )(mix, qg, kg, proj, proj, proj, ck, cv, *tabs128, *tabs128)
    return mix, kn


def _ret_kernel(*refs, seq, latent):
    refs = refs[1:]
    if latent:
        (lgf_ref, lgb_ref, q_ref, k_ref, v_ref, gc_ref, s0f_ref, s0b_ref,
         cq_ref, saq_ref, sbq_ref, ck_ref, sak_ref, sbk_ref, o_ref) = refs
    else:
        lgf_ref, lgb_ref, q_ref, k_ref, v_ref, gc_ref, o_ref, sf_ref, sb_ref = refs
    hp = pl.program_id(1)
    qi = pl.program_id(2)
    q = q_ref[...]
    k = k_ref[...] * (RET_QK_DIM ** -0.5)
    if latent:
        half = RET_QK_DIM // 2
        q = _apply_rope(q, cq_ref[...], saq_ref[...], sbq_ref[...], half)
        k = _apply_rope(k, ck_ref[...], sak_ref[...], sbk_ref[...], half)
    tq = q.shape[0]
    t_idx = qi * tq + lax.broadcasted_iota(jnp.int32, (tq, seq), 0)
    s_idx = lax.broadcasted_iota(jnp.int32, (tq, seq), 1)
    rel = (t_idx - s_idx).astype(F32)
    t_col = (qi * tq + lax.broadcasted_iota(jnp.int32, (tq, 1), 0)).astype(F32)
    s_col = lax.broadcasted_iota(jnp.int32, (seq, 1), 0).astype(F32)
    lane_q = lax.broadcasted_iota(jnp.int32, q.shape, 1)
    lane_k = lax.broadcasted_iota(jnp.int32, k.shape, 1)
    kb = k.astype(BF16)
    for j in range(2):
        lgf = lgf_ref[2 * hp + j]
        lgb = lgb_ref[2 * hp + j]
        decay = (jnp.where(rel >= 0, jnp.exp(jnp.maximum(rel, 0.0) * lgf), 0.0)
                 + jnp.where(rel <= 0, jnp.exp(jnp.maximum(-rel, 0.0) * lgb), 0.0))
        in_head_q = (lane_q >= j * RET_QK_DIM) & (lane_q < (j + 1) * RET_QK_DIM)
        qj = jnp.where(in_head_q, q, 0.0).astype(BF16)
        vj = v_ref[:, j * HEAD_DIM:(j + 1) * HEAD_DIM].astype(BF16)
        scores = lax.dot_general(qj, kb, _NT, preferred_element_type=F32) * decay
        o = jnp.dot(scores.astype(BF16), vj, preferred_element_type=F32)
        if latent:
            o = o + (jnp.dot(qj, s0f_ref[...].astype(BF16), preferred_element_type=F32)
                     * jnp.exp((t_col + 1.0) * lgf))
            o = o + (jnp.dot(qj, s0b_ref[...].astype(BF16), preferred_element_type=F32)
                     * jnp.exp((seq - t_col) * lgb))
        else:
            in_head_k = (lane_k >= j * RET_QK_DIM) & (lane_k < (j + 1) * RET_QK_DIM)
            kj = jnp.where(in_head_k, k, 0.0)
            kf = (kj * jnp.exp((seq - 1.0 - s_col) * lgf)).astype(BF16)
            kr = (kj * jnp.exp(s_col * lgb)).astype(BF16)
            sf = lax.dot_general(kf, vj, _TN, preferred_element_type=F32)
            sb = lax.dot_general(kr, vj, _TN, preferred_element_type=F32)
            sf_ref[0, j] = sf[j * RET_QK_DIM:(j + 1) * RET_QK_DIM]
            sb_ref[0, j] = sb[j * RET_QK_DIM:(j + 1) * RET_QK_DIM]
        mu = jnp.mean(o, axis=-1, keepdims=True)
        oc = o - mu
        var = jnp.mean(oc * oc, axis=-1, keepdims=True)
        gcj = gc_ref[:, j * HEAD_DIM:(j + 1) * HEAD_DIM]
        silu = gcj * (1.0 / (1.0 + jnp.exp(-gcj)))
        o_ref[:, j * HEAD_DIM:(j + 1) * HEAD_DIM] = (oc * lax.rsqrt(var + NORM_EPS) * silu).astype(o_ref.dtype)


def _retention(mix, proj, lg_f, lg_b, layer, state_f, state_b, tabs64):
    pairs = RET_HEADS // 2
    pw = 2 * HEAD_DIM
    smem = pl.BlockSpec(memory_space=pltpu.SMEM)
    vcol = COL_VC * LANES // pw
    gcol = COL_GC * LANES // pw
    mcol = (DIFF_HEADS + GQA_Q_HEADS) * HEAD_DIM // pw
    mix, sf, sb = pl.pallas_call(
        functools.partial(_ret_kernel, seq=SEQ, latent=False),
        grid=(BATCH, pairs, 1),
        in_specs=[
            _ANY, smem, smem,
            pl.BlockSpec((SEQ, LANES), lambda b, p, qi: (b, COL_QC + p)),
            pl.BlockSpec((SEQ, LANES), lambda b, p, qi: (b, COL_KC + p)),
            pl.BlockSpec((SEQ, pw), lambda b, p, qi: (b, vcol + p)),
            pl.BlockSpec((SEQ, pw), lambda b, p, qi: (b, gcol + p)),
        ],
        out_specs=[
            pl.BlockSpec((SEQ, pw), lambda b, p, qi: (b, mcol + p)),
            pl.BlockSpec((1, 2, RET_QK_DIM, HEAD_DIM), lambda b, p, qi: (b, p, 0, 0)),
            pl.BlockSpec((1, 2, RET_QK_DIM, HEAD_DIM), lambda b, p, qi: (b, p, 0, 0)),
        ],
        out_shape=[
            jax.ShapeDtypeStruct((N_TOK, MIX_WIDTH), BF16),
            jax.ShapeDtypeStruct((BATCH, RET_HEADS, RET_QK_DIM, HEAD_DIM), F32),
            jax.ShapeDtypeStruct((BATCH, RET_HEADS, RET_QK_DIM, HEAD_DIM), F32),
        ],
        input_output_aliases={0: 0},
        compiler_params=_params("arbitrary", "arbitrary", "arbitrary"),
        name="ret_ctx",
    )(mix, lg_f, lg_b, proj, proj, proj, proj)

    nq = DEC_SEQ // ATT_TQ
    q_row = lambda b, qi: (N_CTX // ATT_TQ) + b * nq + qi
    kv_row = lambda b: (N_CTX // DEC_SEQ) + b
    s0f = state_f.reshape(DEC_BATCH, DEPTH, pairs, 2 * RET_QK_DIM, HEAD_DIM)
    s0b = state_b.reshape(DEC_BATCH, DEPTH, pairs, 2 * RET_QK_DIM, HEAD_DIM)
    tab_q = pl.BlockSpec((ATT_TQ, LANES), lambda b, p, qi: (qi, 0))
    tab_k = pl.BlockSpec((DEC_SEQ, LANES), lambda b, p, qi: (0, 0))
    state_spec = pl.BlockSpec((None, None, None, 2 * RET_QK_DIM, HEAD_DIM), lambda b, p, qi: (b, layer, p, 0, 0))
    mix = pl.pallas_call(
        functools.partial(_ret_kernel, seq=DEC_SEQ, latent=True),
        grid=(DEC_BATCH, pairs, nq),
        in_specs=[
            _ANY, smem, smem,
            pl.BlockSpec((ATT_TQ, LANES), lambda b, p, qi: (q_row(b, qi), COL_QC + p)),
            pl.BlockSpec((DEC_SEQ, LANES), lambda b, p, qi: (kv_row(b), COL_KC + p)),
            pl.BlockSpec((DEC_SEQ, pw), lambda b, p, qi: (kv_row(b), vcol + p)),
            pl.BlockSpec((ATT_TQ, pw), lambda b, p, qi: (q_row(b, qi), gcol + p)),
            state_spec, state_spec,
            tab_q, tab_q, tab_q, tab_k, tab_k, tab_k,
        ],
        out_specs=pl.BlockSpec((ATT_TQ, pw), lambda b, p, qi: (q_row(b, qi), mcol + p)),
        out_shape=jax.ShapeDtypeStruct((N_TOK, MIX_WIDTH), BF16),
        input_output_aliases={0: 0},
        compiler_params=_params("arbitrary", "arbitrary", "arbitrary"),
        name="ret_lat",
    )(mix, lg_f, lg_b, proj, proj, proj, proj, s0f, s0b, *tabs64, *tabs64)
    return mix, sf, sb


def _router_kernel(x_ref, g_ref, sh_ref, sc_ref, rw_ref, rb_ref, h_ref, e_ref, gate_ref):
    h = _ada_norm_value(x_ref[...], g_ref[...], sh_ref[0], sc_ref[0])
    h_ref[...] = h
    hh = h.astype(BF16)
    hl = (h - hh.astype(F32)).astype(BF16)
    w = rw_ref[...]
    wh = w.astype(BF16)
    wl = (w - wh.astype(F32)).astype(BF16)
    logits = (jnp.dot(hh, wh, preferred_element_type=F32) + jnp.dot(hh, wl, preferred_element_type=F32)
              + jnp.dot(hl, wh, preferred_element_type=F32) + rb_ref[...])
    lane = lax.broadcasted_iota(jnp.int32, logits.shape, 1).astype(F32)
    wide = lax.broadcasted_iota(jnp.int32, e_ref.shape, 1)
    e_out = jnp.zeros(e_ref.shape, F32)
    v_out = jnp.zeros(e_ref.shape, F32)
    top0 = None
    den = None
    for kk in range(TOP_K):
        m = jnp.max(logits, axis=-1, keepdims=True)
        idx = jnp.min(jnp.where(logits == m, lane, float(N_EXPERTS)), axis=-1, keepdims=True)
        logits = jnp.where(lane == idx, -jnp.inf, logits)
        if kk == 0:
            top0 = m
        ex = jnp.exp(m - top0)
        den = ex if den is None else den + ex
        e_out = jnp.where(wide == kk, idx, e_out)
        v_out = jnp.where(wide == kk, ex, v_out)
    e_ref[...] = e_out.astype(jnp.int32)
    gate_ref[...] = v_out / den


def _router(x, g, modt, layer, router_w, router_b):
    tm = 256
    sh0 = _mod_index(layer, 3)
    sc0 = _mod_index(layer, 4)
    return pl.pallas_call(
        _router_kernel,
        grid=(N_TOK // tm,),
        in_specs=[
            pl.BlockSpec((tm, D_MODEL), lambda i: (i, 0)),
            pl.BlockSpec((1, D_MODEL), lambda i: (0, 0)),
            pl.BlockSpec((1, 1, D_MODEL), lambda i: (sh0 + _stream_of_row_tile(i, tm), 0, 0)),
            pl.BlockSpec((1, 1, D_MODEL), lambda i: (sc0 + _stream_of_row_tile(i, tm), 0, 0)),
            pl.BlockSpec((None, D_MODEL, N_EXPERTS), lambda i: (layer, 0, 0)),
            pl.BlockSpec((None, 1, N_EXPERTS), lambda i: (layer, 0, 0)),
        ],
        out_specs=[
            pl.BlockSpec((tm, D_MODEL), lambda i: (i, 0)),
            pl.BlockSpec((tm, LANES), lambda i: (i, 0)),
            pl.BlockSpec((tm, LANES), lambda i: (i, 0)),
        ],
        out_shape=[
            jax.ShapeDtypeStruct((N_TOK, D_MODEL), F32),
            jax.ShapeDtypeStruct((N_TOK, LANES), jnp.int32),
            jax.ShapeDtypeStruct((N_TOK, LANES), F32),
        ],
        compiler_params=_params("arbitrary"),
        name="router",
    )(x, g.reshape(1, D_MODEL), modt, modt, router_w, router_b.reshape(DEPTH, 1, N_EXPERTS))


def _route_meta(top_e):
    n_assign = N_TOK * TOP_K
    n_items = -(-n_assign // MOE_ROWS) + N_EXPERTS
    flat_e = top_e.reshape(n_assign)
    order = jnp.argsort(flat_e).astype(jnp.int32)
    inv = jnp.argsort(order).astype(jnp.int32)
    onehot = (flat_e[:, None] == jnp.arange(N_EXPERTS, dtype=jnp.int32)[None, :]).astype(jnp.int32)
    counts = jnp.sum(onehot, axis=0)
    padded = (counts + MOE_ROWS - 1) // MOE_ROWS * MOE_ROWS
    start = jnp.cumsum(counts) - counts
    pad_end = jnp.cumsum(padded)
    pad_start = pad_end - padded
    slot_of = inv + jnp.sum(onehot * (pad_start - start)[None, :], axis=1)
    tok_sorted = jnp.concatenate([order // TOP_K, jnp.zeros((8,), jnp.int32)])
    item_start = jnp.arange(n_items, dtype=jnp.int32) * MOE_ROWS
    item_e = jnp.minimum(jnp.searchsorted(pad_end, item_start, side='right'), N_EXPERTS - 1).astype(jnp.int32)
    off = item_start - pad_start[item_e]
    item_rows = jnp.clip(counts[item_e] - off, 0, MOE_ROWS).astype(jnp.int32)
    item_rows8 = (item_rows + 7) // 8 * 8
    item_src = jnp.clip(start[item_e] + off, 0, n_assign - 1).astype(jnp.int32)
    n_act = (pad_end[-1] // MOE_ROWS).astype(jnp.int32).reshape(1)
    return tok_sorted, slot_of.astype(jnp.int32), (item_e, item_rows, item_rows8, item_src, n_act), n_items


def _expert_kernel(ie_ref, rows_ref, rows8_ref, src_ref, nact_ref, tok_ref,
                   h_hbm, wg_ref, wu_ref, bg_ref, bu_ref, wd_ref, bd_ref, o_ref, x_ref, act_ref, sem):
    it = pl.program_id(0)
    s = pl.program_id(1)
    n_act = nact_ref[0]
    active = it < n_act
    rows = rows_ref[it]
    n_full = rows // MOE_SUB
    rem = rows - n_full * MOE_SUB
    n_main = n_full + (rem > MOE_SUB // 2).astype(jnp.int32)
    tail0 = pl.multiple_of(n_full * MOE_SUB, MOE_SUB)
    short_tail = (rem > 0) & (rem <= MOE_SUB // 4)
    half_tail = (rem > MOE_SUB // 4) & (rem <= MOE_SUB // 2)

    def row_copy(tok, r):
        return pltpu.make_async_copy(h_hbm.at[pl.ds(tok, 1), :], x_ref.at[pl.ds(r, 1), :], sem.at[0])

    def issue(item, lo, hi):
        base = src_ref[item]

        def body(g, carry):
            r0 = pl.multiple_of(g * 8, 8)
            for u in range(8):
                row_copy(tok_ref[base + r0 + u], r0 + u).start()
            return carry
        lax.fori_loop(lo // 8, hi // 8, body, 0)

    @pl.when(active & (s == 0))
    def _():
        @pl.when(it == 0)
        def _():
            x_ref[...] = jnp.zeros(x_ref.shape, F32)
            issue(0, 0, rows8_ref[0])

        def wait_body(g, carry):
            for u in range(8):
                row_copy(0, 0).wait()
            return carry
        lax.fori_loop(0, rows8_ref[it] // 8, wait_body, 0)

    def gate_up(r0, m):
        xs = x_ref[pl.ds(r0, m), :].astype(BF16)
        gate = jnp.dot(xs, wg_ref[...].astype(BF16), preferred_element_type=F32) + bg_ref[...]
        up = jnp.dot(xs, wu_ref[...].astype(BF16), preferred_element_type=F32) + bu_ref[...]
        gate = jnp.minimum(gate, SWIGLU_LIMIT)
        up = jnp.clip(up, -SWIGLU_LIMIT, SWIGLU_LIMIT)
        a = (up + 1.0) * gate * (1.0 / (1.0 + jnp.exp(-SWIGLU_ALPHA * gate)))
        act_ref[s, pl.ds(r0, m), :] = a.astype(BF16)

    def down(r0, m):
        acc = None
        for f in range(MOE_NF):
            d = jnp.dot(act_ref[f, pl.ds(r0, m), :], wd_ref[f * MOE_TF:(f + 1) * MOE_TF, :].astype(BF16),
                        preferred_element_type=F32)
            acc = d if acc is None else acc + d
        o_ref[pl.ds(r0, m), :] = acc + bd_ref[...]

    def run_rows(fn):
        def body(i, carry):
            fn(pl.multiple_of(i * 2 * MOE_SUB, 2 * MOE_SUB), 2 * MOE_SUB)
            return carry
        lax.fori_loop(0, n_main // 2, body, 0)

        @pl.when(n_main % 2 == 1)
        def _():
            fn(pl.multiple_of((n_main - 1) * MOE_SUB, MOE_SUB), MOE_SUB)

        @pl.when(short_tail)
        def _():
            fn(tail0, MOE_SUB // 4)

        @pl.when(half_tail)
        def _():
            fn(tail0, MOE_SUB // 2)

    @pl.when(active & (s < MOE_NF))
    def _():
        run_rows(gate_up)

    @pl.when(active & (s >= MOE_NF))
    def _():
        @pl.when(it + 1 < n_act)
        def _():
            d = s - MOE_NF
            issue(it + 1, d * MOE_ISSUE, jnp.minimum((d + 1) * MOE_ISSUE, rows8_ref[it + 1]))
        run_rows(down)


def _experts(h, tok_sorted, item_meta, n_items, layer, w_gu, b_gu, w_dn, b_dn):
    n_steps = MOE_NF + MOE_ND
    cap = n_items * MOE_ROWS

    def pos(it, s, ie, nact):
        last = jnp.maximum(nact[0] - 1, 0)
        itc = jnp.minimum(it, last)
        sc = jnp.where(it < nact[0], s, n_steps - 1)
        return itc, sc, ie[itc]

    def wg_map(it, s, ie, rows, rows8, src, nact, tok):
        _, sc, e = pos(it, s, ie, nact)
        return (layer, e, 0, jnp.minimum(sc, MOE_NF - 1))

    def wu_map(it, s, ie, rows, rows8, src, nact, tok):
        _, sc, e = pos(it, s, ie, nact)
        return (layer, e, 0, MOE_NF + jnp.minimum(sc, MOE_NF - 1))

    def wd_map(it, s, ie, rows, rows8, src, nact, tok):
        _, sc, e = pos(it, s, ie, nact)
        return (layer, e, 0, jnp.clip(sc - MOE_NF, 0, MOE_ND - 1))

    def o_map(it, s, ie, rows, rows8, src, nact, tok):
        itc, sc, _ = pos(it, s, ie, nact)
        return (itc, jnp.clip(sc - MOE_NF, 0, MOE_ND - 1))

    grid_spec = pltpu.PrefetchScalarGridSpec(
        num_scalar_prefetch=6,
        grid=(n_items, n_steps),
        in_specs=[
            _ANY,
            pl.BlockSpec((None, None, D_MODEL, MOE_TF), wg_map),
            pl.BlockSpec((None, None, D_MODEL, MOE_TF), wu_map),
            pl.BlockSpec((None, None, 1, MOE_TF), wg_map),
            pl.BlockSpec((None, None, 1, MOE_TF), wu_map),
            pl.BlockSpec((None, None, EXPERT_FF, MOE_TD), wd_map),
            pl.BlockSpec((None, None, 1, MOE_TD), wd_map),
        ],
        out_specs=pl.BlockSpec((MOE_ROWS, MOE_TD), o_map),
        scratch_shapes=[
            pltpu.VMEM((MOE_ROWS, D_MODEL), F32),
            pltpu.VMEM((MOE_NF, MOE_ROWS, MOE_TF), BF16),
            pltpu.SemaphoreType.DMA((1,)),
        ],
    )
    b_gu4 = b_gu.reshape(DEPTH, N_EXPERTS, 1, 2 * EXPERT_FF)
    return pl.pallas_call(
        _expert_kernel,
        grid_spec=grid_spec,
        out_shape=jax.ShapeDtypeStruct((cap, D_MODEL), F32),
        compiler_params=_params("arbitrary", "arbitrary"),
        name="moe_experts",
    )(*item_meta, tok_sorted, h, w_gu, w_gu, b_gu4, b_gu4, w_dn, b_dn.reshape(DEPTH, N_EXPERTS, 1, D_MODEL))


def _combine_kernel(slot_ref, y_hbm, x_ref, gates_ref, g2_ref, fg_ref, o_ref, buf, sem, *, final_norm):
    i = pl.program_id(0)
    n = pl.num_programs(0)

    def copy(tile, slot, r, kk):
        row = slot_ref[(tile * COMBINE_TOK + r) * TOP_K + kk]
        return pltpu.make_async_copy(y_hbm.at[pl.ds(row, 1), :], buf.at[slot, kk, pl.ds(r, 1), :], sem.at[slot])

    def issue(tile, slot):
        def body(r, carry):
            for kk in range(TOP_K):
                copy(tile, slot, r, kk).start(priority=kk % 2)
            return carry
        lax.fori_loop(0, COMBINE_TOK, body, 0)

    @pl.when(i == 0)
    def _():
        issue(0, 0)

    @pl.when(i + 1 < n)
    def _():
        issue(i + 1, (i + 1) % 2)

    slot = i % 2

    def wait_body(r, carry):
        for kk in range(TOP_K):
            copy(i, slot, r, kk).wait()
        return carry
    lax.fori_loop(0, COMBINE_TOK, wait_body, 0)

    gates = gates_ref[...]
    y = gates[:, 0:1] * buf[slot, 0]
    for kk in range(1, TOP_K):
        y = y + gates[:, kk:kk + 1] * buf[slot, kk]
    out = x_ref[...] + g2_ref[0] * y
    if final_norm:
        ms = jnp.mean(out * out, axis=-1, keepdims=True)
        out = out * lax.rsqrt(ms + NORM_EPS) * fg_ref[...]
    o_ref[...] = out


def _combine(yb, slot_of, x, gates, modt, layer, final_g, final_norm):
    tc = COMBINE_TOK
    g0 = _mod_index(layer, 5)
    grid_spec = pltpu.PrefetchScalarGridSpec(
        num_scalar_prefetch=1,
        grid=(N_TOK // tc,),
        in_specs=[
            pl.BlockSpec(memory_space=pl.ANY),
            pl.BlockSpec((tc, D_MODEL), lambda i, sl: (i, 0)),
            pl.BlockSpec((tc, LANES), lambda i, sl: (i, 0)),
            pl.BlockSpec((1, 1, D_MODEL), lambda i, sl: (g0 + _stream_of_row_tile(i, tc), 0, 0)),
            pl.BlockSpec((1, D_MODEL), lambda i, sl: (0, 0)),
        ],
        out_specs=pl.BlockSpec((tc, D_MODEL), lambda i, sl: (i, 0)),
        scratch_shapes=[pltpu.VMEM((2, TOP_K, tc, D_MODEL), F32), pltpu.SemaphoreType.DMA((2,))],
    )
    return pl.pallas_call(
        functools.partial(_combine_kernel, final_norm=final_norm),
        grid_spec=grid_spec,
        out_shape=jax.ShapeDtypeStruct((N_TOK, D_MODEL), F32),
        compiler_params=_params("arbitrary"),
        name="moe_combine",
    )(slot_of, yb, x, gates, modt, final_g.reshape(1, D_MODEL))


def kernel(x_prompt, x_sample, cache_diff_k, cache_diff_v, cache_gqa_k, cache_gqa_v, state_ret_fwd, state_ret_bwd, c, c_ctx, w_mod, b_mod, norm_mix_g, norm_ffn_g, w_in, w_out, diff_lambda, diff_subln_g, gqa_q_norm_g, gqa_k_norm_g, ret_log_decay_fwd, ret_log_decay_bwd, router_w, router_b, moe_w_gate_up, moe_b_gate_up, moe_w_down, moe_b_down, final_norm_g):
    x = jnp.concatenate([x_prompt.reshape(N_CTX, D_MODEL), x_sample.reshape(N_LAT, D_MODEL)], axis=0)
    cond_raw = jnp.concatenate([c_ctx[None].astype(F32), c.astype(F32),
                                jnp.zeros((8 - N_STREAM, D_MODEL), F32)], axis=0)
    modt = _modulation(cond_raw, w_mod, b_mod)
    tabs64 = _rope_tables(DEC_SEQ, DIFF_QK_DIM)
    tabs128 = _rope_tables(DEC_SEQ, HEAD_DIM)

    new_dk, new_dv, new_gk, new_gv, new_sf, new_sb = [], [], [], [], [], []
    for layer in range(DEPTH):
        lam_init = 0.8 - 0.6 * math.exp(-0.3 * layer)
        h = _adanorm(x, norm_mix_g[layer], modt, layer, 0, 1)
        proj = _in_proj(h, w_in[layer])
        mix = _diff_attention(proj, diff_lambda[layer], diff_subln_g[layer], lam_init, layer,
                              cache_diff_k, cache_diff_v, tabs64)
        mix, kn = _gqa_attention(mix, proj, gqa_q_norm_g[layer], gqa_k_norm_g[layer], layer,
                                 cache_gqa_k, cache_gqa_v, tabs128)
        mix, sf, sb = _retention(mix, proj, ret_log_decay_fwd[layer], ret_log_decay_bwd[layer], layer,
                                 state_ret_fwd, state_ret_bwd, tabs64)
        x = _out_proj(mix, w_out[layer], x, modt, layer, 2)

        h2, top_e, gates = _router(x, norm_ffn_g[layer], modt, layer, router_w, router_b)
        tok_sorted, slot_of, item_meta, n_items = _route_meta(top_e[:, :TOP_K])
        yb = _experts(h2, tok_sorted, item_meta, n_items, layer,
                      moe_w_gate_up, moe_b_gate_up, moe_w_down, moe_b_down)
        x = _combine(yb, slot_of, x, gates, modt, layer, final_norm_g, layer == DEPTH - 1)

        pc = proj[:N_CTX]
        new_dk.append(pc[:, COL_KA * LANES:COL_VA * LANES].reshape(BATCH, SEQ, DIFF_HEADS, 2, DIFF_QK_DIM))
        new_dv.append(pc[:, COL_VA * LANES:COL_QB * LANES].reshape(BATCH, SEQ, DIFF_HEADS, HEAD_DIM))
        new_gk.append(kn.reshape(BATCH, SEQ, GQA_KV_HEADS, HEAD_DIM))
        new_gv.append(pc[:, COL_VB * LANES:COL_QC * LANES].reshape(BATCH, SEQ, GQA_KV_HEADS, HEAD_DIM))
        new_sf.append(sf)
        new_sb.append(sb)

    y_prompt = x[:N_CTX].reshape(BATCH, SEQ, D_MODEL)
    y_sample = x[N_CTX:].reshape(DEC_BATCH, DEC_SEQ, D_MODEL)
    return (y_prompt, y_sample, jnp.stack(new_dk, axis=1), jnp.stack(new_dv, axis=1),
            jnp.stack(new_gk, axis=1), jnp.stack(new_gv, axis=1),
            jnp.stack(new_sf, axis=1), jnp.stack(new_sb, axis=1))
```
